```python
import jax, jax.numpy as jnp
from jax import lax
import numpy as np

D_MODEL = 1024
BATCH = 8
SEQ = 4096
DEPTH = 2

HEAD_DIM = 64
BLOCK_Q = 128
ROPE_THETA = 10000.0
EPS = 1e-6
N_BRANCH = 3
FOX_HEADS = 8
FOX_WIDTH = FOX_HEADS * HEAD_DIM
DSA_HEADS = 8
DSA_WIDTH = DSA_HEADS * HEAD_DIM
DSA_KV_RANK = 128
IDX_HEADS = 8
IDX_DIM = 64
DSA_TOPK_MAX = 256
SB_HEADS = 8
SB_WIDTH = SB_HEADS * HEAD_DIM

IN_SIZES = (FOX_WIDTH, FOX_WIDTH, FOX_WIDTH, FOX_HEADS, FOX_WIDTH,
            DSA_WIDTH, DSA_KV_RANK, IDX_HEADS * IDX_DIM, IDX_DIM, IDX_HEADS, DSA_WIDTH,
            SB_WIDTH, SB_WIDTH, SB_WIDTH, SB_WIDTH,
            N_BRANCH * D_MODEL)
N_IN = sum(IN_SIZES)

kernel_name = "hybrid_fox_dsa_stickbreak_gated_block"


def rms_norm(x, g):
    x32 = x.astype(jnp.float32)
    y = x32 * lax.rsqrt(jnp.mean(x32 * x32, axis=-1, keepdims=True) + EPS)
    return (y * g.astype(jnp.float32)).astype(x.dtype)


def rope(x, positions):
    half = x.shape[-1] // 2
    inv_freq = ROPE_THETA ** (-jnp.arange(half, dtype=jnp.float32) / half)
    ang = positions.astype(jnp.float32)[..., None] * inv_freq
    if x.ndim == 4:
        ang = ang[:, :, None, :]
    cos, sin = jnp.cos(ang), jnp.sin(ang)
    x32 = x.astype(jnp.float32)
    x1, x2 = x32[..., :half], x32[..., half:]
    return jnp.concatenate([x1 * cos - x2 * sin, x2 * cos + x1 * sin], axis=-1).astype(x.dtype)


def fox_attention(q, k, v, log_f):
    S, d = q.shape[1], q.shape[3]
    scale = d ** -0.5
    F = jnp.cumsum(log_f, axis=1).transpose(0, 2, 1)
    outs = []
    for i in range(S // BLOCK_Q):
        q0, q1 = i * BLOCK_Q, (i + 1) * BLOCK_Q
        s = jnp.einsum('bqhd,bkhd->bhqk', q[:, q0:q1], k[:, :q1]).astype(jnp.float32) * scale
        bias = F[:, :, q0:q1, None] - F[:, :, None, :q1]
        mask = jnp.arange(q0, q1)[:, None] >= jnp.arange(q1)[None, :]
        p = jax.nn.softmax(jnp.where(mask, s + bias, -jnp.inf), axis=-1)
        outs.append(jnp.einsum('bhqk,bkhd->bqhd', p.astype(v.dtype), v[:, :q1]))
    return jnp.concatenate(outs, axis=1)


def dsa_attention(q, k, v, iq, ik, iw, k_top):
    S, d = q.shape[1], q.shape[3]
    scale = d ** -0.5
    idx_scale = IDX_DIM ** -0.5
    gather = jax.vmap(lambda a, i: a[i])
    outs = []
    for i in range(S // BLOCK_Q):
        q0, q1 = i * BLOCK_Q, (i + 1) * BLOCK_Q
        kend = max(q1, k_top)
        qpos = jnp.arange(q0, q1)
        si = jnp.einsum('bqhd,bkd->bqhk', iq[:, q0:q1], ik[:, :kend]).astype(jnp.float32) * idx_scale
        score = jnp.einsum('bqhk,bqh->bqk', jax.nn.relu(si), iw[:, q0:q1].astype(jnp.float32))
        score = jnp.where(jnp.arange(kend)[None, None, :] <= qpos[None, :, None], score, -jnp.inf)
        _, sel = lax.top_k(score, k_top)
        valid = sel <= qpos[None, :, None]
        k_sel = gather(k, sel)
        v_sel = gather(v, sel)
        s = jnp.einsum('bqhd,bqkd->bqhk', q[:, q0:q1], k_sel).astype(jnp.float32) * scale
        p = jax.nn.softmax(jnp.where(valid[:, :, None, :], s, -jnp.inf), axis=-1)
        outs.append(jnp.einsum('bqhk,bqkd->bqhd', p.astype(v.dtype), v_sel))
    return jnp.concatenate(outs, axis=1)


def stick_breaking_attention(q, k, v):
    S, d = q.shape[1], q.shape[3]
    scale = d ** -0.5
    outs = []
    for i in range(S // BLOCK_Q):
        q0, q1 = i * BLOCK_Q, (i + 1) * BLOCK_Q
        z = jnp.einsum('bqhd,bkhd->bhqk', q[:, q0:q1], k[:, :q1]).astype(jnp.float32) * scale
        mask = jnp.arange(q0, q1)[:, None] > jnp.arange(q1)[None, :]
        log_1m = jnp.where(mask, jax.nn.log_sigmoid(-z), 0.0)
        between = lax.cumsum(log_1m, axis=3, reverse=True) - log_1m
        a = jnp.where(mask, jnp.exp(jax.nn.log_sigmoid(z) + between), 0.0)
        outs.append(jnp.einsum('bhqk,bkhd->bqhd', a.astype(v.dtype), v[:, :q1]))
    return jnp.concatenate(outs, axis=1)


def mixer_layer(x, c, positions, w_ada, b_ada, g_norm, w_in, b_fgt, g_kv, w_kv_up,
                w_br_fox, w_br_dsa, w_br_sb, w_out):
    B, S, _ = x.shape
    k_top = min(DSA_TOPK_MAX, S // 4)
    mod = jax.nn.silu(c) @ w_ada + b_ada
    shift, scale, gate = jnp.split(mod, 3, axis=-1)
    h = rms_norm(x, g_norm) * (1.0 + scale[:, None, :]) + shift[:, None, :]
    z = h @ w_in
    points = [int(p) for p in np.cumsum(IN_SIZES)[:-1]]
    (fq, fk, fv, ff, fg,
     dq, dckv, diq, dik, diw, dg,
     sq, sk, sv, sg, merge) = jnp.split(z, points, axis=-1)
    heads = lambda t, n: t.reshape(B, S, n, HEAD_DIM)

    log_f = jax.nn.log_sigmoid((ff + b_fgt).astype(jnp.float32))
    y_fox = fox_attention(heads(fq, FOX_HEADS), heads(fk, FOX_HEADS), heads(fv, FOX_HEADS), log_f)
    y_fox = y_fox.reshape(B, S, FOX_WIDTH) * jax.nn.silu(fg)

    kv = rms_norm(dckv, g_kv) @ w_kv_up
    dk, dv = kv[..., :HEAD_DIM], kv[..., HEAD_DIM:]
    q_d = rope(heads(dq, DSA_HEADS), positions)
    dk = rope(dk, positions)
    iq = rope(diq.reshape(B, S, IDX_HEADS, IDX_DIM), positions)
    ik = rope(dik, positions)
    iw = diw * (IDX_HEADS ** -0.5)
    y_dsa = dsa_attention(q_d, dk, dv, iq, ik, iw, k_top)
    y_dsa = y_dsa.reshape(B, S, DSA_WIDTH) * jax.nn.silu(dg)

    y_sb = stick_breaking_attention(heads(sq, SB_HEADS), heads(sk, SB_HEADS), heads(sv, SB_HEADS))
    y_sb = y_sb.reshape(B, S, SB_WIDTH) * jax.nn.silu(sg)

    m_fox, m_dsa, m_sb = jnp.split(jax.nn.sigmoid(merge), N_BRANCH, axis=-1)
    mixed = m_fox * (y_fox @ w_br_fox) + m_dsa * (y_dsa @ w_br_dsa) + m_sb * (y_sb @ w_br_sb)
    return x + gate[:, None, :] * (mixed @ w_out)


def setup_inputs(seed: int = 0) -> dict:
    key = jax.random.key(seed)
    ks = jax.random.split(key, 16)
    nrm = lambda k, shape, s: jax.random.normal(k, shape, jnp.float32) * s
    D = D_MODEL
    x = nrm(ks[0], (BATCH, SEQ, D), 1.0)
    c = nrm(ks[1], (BATCH, D), 1.0)
    offsets = jax.random.randint(ks[2], (BATCH, 1), 0, SEQ, dtype=jnp.int32)
    positions = jnp.arange(SEQ, dtype=jnp.int32)[None, :] + offsets
    return {
        "x": x,
        "c": c,
        "positions": positions,
        "w_ada": nrm(ks[3], (DEPTH, D, 3 * D), 0.5 * D ** -0.5),
        "b_ada": nrm(ks[4], (DEPTH, 3 * D), 0.02),
        "g_norm": 1.0 + nrm(ks[5], (DEPTH, D), 0.02),
        "w_in": nrm(ks[6], (DEPTH, D, N_IN), D ** -0.5),
        "b_fgt": 2.0 + nrm(ks[7], (DEPTH, FOX_HEADS), 0.5),
        "g_kv": 1.0 + nrm(ks[8], (DEPTH, DSA_KV_RANK), 0.02),
        "w_kv_up": nrm(ks[9], (DEPTH, DSA_KV_RANK, 2 * HEAD_DIM), DSA_KV_RANK ** -0.5),
        "w_br_fox": nrm(ks[10], (DEPTH, FOX_WIDTH, D), FOX_WIDTH ** -0.5),
        "w_br_dsa": nrm(ks[11], (DEPTH, DSA_WIDTH, D), DSA_WIDTH ** -0.5),
        "w_br_sb": nrm(ks[12], (DEPTH, SB_WIDTH, D), SB_WIDTH ** -0.5),
        "w_out": nrm(ks[13], (DEPTH, D, D), D ** -0.5),
        "g_final": 1.0 + nrm(ks[14], (D,), 0.02),
    }


def reference(x, c, positions, w_ada, b_ada, g_norm, w_in, b_fgt, g_kv, w_kv_up,
              w_br_fox, w_br_dsa, w_br_sb, w_out, g_final):
    for l in range(DEPTH):
        x = mixer_layer(x, c, positions, w_ada[l], b_ada[l], g_norm[l], w_in[l], b_fgt[l],
                        g_kv[l], w_kv_up[l], w_br_fox[l], w_br_dsa[l], w_br_sb[l], w_out[l])
    return rms_norm(x, g_final)
```

```python
import functools

import jax
import jax.numpy as jnp
from jax import lax
from jax.experimental import pallas as pl
from jax.experimental.pallas import tpu as pltpu

F32 = jnp.float32
BF16 = jnp.bfloat16
I32 = jnp.int32
HIGHEST = lax.Precision.HIGHEST

HEAD_DIM = 64
N_HEADS = 8
WIDTH = N_HEADS * HEAD_DIM
KV_RANK = 128
TOPK_MAX = 256
ROPE_THETA = 10000.0
EPS = 1e-6
LANES = 128
NEG = -1e30
INT_MIN = -(2 ** 31)
VMEM_LIMIT = 56 * 1024 * 1024

_MERGE, _FQ, _FK, _FV, _FG, _DQ, _DQR, _DG, _SQ, _SK, _SV, _SG = 0, 6, 7, 8, 9, 10, 11, 12, 13, 14, 15, 16
N_MAIN = 17 * WIDTH
_IQ, _IQR = 0, 1
_MISC, _CKV, _IK, _IKR = 8, 9, 10, 11
N_SMALL = 12 * LANES

_NT = (((1,), (1,)), ((), ()))


def _params(*sem):
    return pltpu.CompilerParams(dimension_semantics=sem, vmem_limit_bytes=VMEM_LIMIT)


def _mod_kernel(c_ref, w_ref, b_ref, o_ref):
    c = c_ref[...]
    sc = c * jax.nn.sigmoid(c)
    o_ref[0] = jnp.dot(sc, w_ref[0], preferred_element_type=F32, precision=HIGHEST) + b_ref[0]


def _modulation(c, w_ada, b_ada):
    L, D, N = w_ada.shape
    B = c.shape[0]
    tn = 512
    return pl.pallas_call(
        _mod_kernel,
        grid=(L, N // tn),
        in_specs=[pl.BlockSpec((B, D), lambda l, n: (0, 0)),
                  pl.BlockSpec((1, D, tn), lambda l, n: (l, 0, n)),
                  pl.BlockSpec((1, 1, tn), lambda l, n: (l, 0, n))],
        out_specs=pl.BlockSpec((1, B, tn), lambda l, n: (l, 0, n)),
        out_shape=jax.ShapeDtypeStruct((L, B, N), F32),
        compiler_params=_params("arbitrary", "arbitrary"),
        name="adaln_mod",
    )(c, w_ada, b_ada.reshape(L, 1, N))


def _norm_proj_kernel(x_ref, g_ref, sc_ref, sh_ref, w_ref, o_ref, h_ref):
    @pl.when(pl.program_id(1) == 0)
    def _():
        x = x_ref[...]
        y = x * lax.rsqrt(jnp.mean(x * x, axis=-1, keepdims=True) + EPS)
        h = (y * g_ref[...]) * (1.0 + sc_ref[0]) + sh_ref[0]
        h_ref[...] = h.astype(BF16)

    o_ref[...] = jnp.dot(h_ref[...], w_ref[...], preferred_element_type=F32).astype(o_ref.dtype)


def _norm_proj(x2, g, scale, shift, w, S, out_dtype, tm, tn, name):
    M, D = x2.shape
    N = w.shape[1]
    per_b = S // tm
    return pl.pallas_call(
        _norm_proj_kernel,
        grid=(M // tm, N // tn),
        in_specs=[pl.BlockSpec((tm, D), lambda m, n: (m, 0)),
                  pl.BlockSpec((1, D), lambda m, n: (0, 0)),
                  pl.BlockSpec((1, 1, D), lambda m, n: (m // per_b, 0, 0)),
                  pl.BlockSpec((1, 1, D), lambda m, n: (m // per_b, 0, 0)),
                  pl.BlockSpec((D, tn), lambda m, n: (0, n))],
        out_specs=pl.BlockSpec((tm, tn), lambda m, n: (m, n)),
        out_shape=jax.ShapeDtypeStruct((M, N), out_dtype),
        scratch_shapes=[pltpu.VMEM((tm, D), BF16)],
        compiler_params=_params("arbitrary", "arbitrary"),
        name=name,
    )(x2, g, scale, shift, w)


def _prep_kernel(misc_ref, ckv_ref, ik_ref, ikr_ref, cos_ref, sin_ref, bf_ref, gkv_ref, wkv_ref,
                 ft_ref, kk_ref, vv_ref, ikk_ref, carry_ref):
    @pl.when(pl.program_id(1) == 0)
    def _():
        carry_ref[...] = jnp.zeros_like(carry_ref)

    tp = misc_ref.shape[1]
    xf = misc_ref[0] + bf_ref[...]
    logf = jnp.minimum(xf, 0.0) - jnp.log1p(jnp.exp(-jnp.abs(xf)))
    r = lax.broadcasted_iota(I32, (tp, tp), 0)
    c = lax.broadcasted_iota(I32, (tp, tp), 1)
    tri = jnp.where(r >= c, 1.0, 0.0).astype(F32)
    cum = jnp.dot(tri, logf, preferred_element_type=F32, precision=HIGHEST) + carry_ref[...]
    carry_ref[...] = cum[tp - 1:tp, :]
    ft_ref[0] = cum.T[:N_HEADS, :]

    cos = cos_ref[0]
    sin = sin_ref[0]
    ckv = ckv_ref[0]
    nrm = ckv * lax.rsqrt(jnp.mean(ckv * ckv, axis=-1, keepdims=True) + EPS) * gkv_ref[...]
    kv = jnp.dot(nrm.astype(BF16), wkv_ref[...], preferred_element_type=F32)
    kk_ref[0] = (kv[:, :LANES] * cos + kv[:, LANES:2 * LANES] * sin).astype(BF16)
    vv_ref[0] = kv[:, 2 * LANES:].astype(BF16)
    ikk_ref[0] = (ik_ref[0] * cos + ikr_ref[0] * sin).astype(BF16)


def _prep(zs, cos, sin, bf, gkv, wkv, tp):
    B, S, _ = zs.shape
    blk = lambda col: pl.BlockSpec((1, tp, LANES), lambda b, j: (b, j, col))
    row = pl.BlockSpec((1, tp, LANES), lambda b, j: (b, j, 0))
    return pl.pallas_call(
        _prep_kernel,
        grid=(B, S // tp),
        in_specs=[blk(_MISC), blk(_CKV), blk(_IK), blk(_IKR), row, row,
                  pl.BlockSpec((1, LANES), lambda b, j: (0, 0)),
                  pl.BlockSpec((1, LANES), lambda b, j: (0, 0)),
                  pl.BlockSpec((KV_RANK, 3 * LANES), lambda b, j: (0, 0))],
        out_specs=[pl.BlockSpec((1, N_HEADS, tp), lambda b, j: (b, 0, j)),
                   row, row, row],
        out_shape=[jax.ShapeDtypeStruct((B, N_HEADS, S), F32),
                   jax.ShapeDtypeStruct((B, S, LANES), BF16),
                   jax.ShapeDtypeStruct((B, S, LANES), BF16),
                   jax.ShapeDtypeStruct((B, S, LANES), BF16)],
        scratch_shapes=[pltpu.VMEM((1, LANES), F32)],
        compiler_params=_params("arbitrary", "arbitrary"),
        name="prep",
    )(zs, zs, zs, zs, cos, sin, bf, gkv, wkv)


def _head_masks():
    lane = lax.broadcasted_iota(I32, (1, LANES), 1)
    lo = lane < HEAD_DIM
    return lo, jnp.logical_not(lo)


def _silu(g):
    return g * jax.nn.sigmoid(g)


def _fox_kernel(q_ref, k_ref, v_ref, g_ref, ft_ref, o_ref, *, tq, tk):
    qi = pl.program_id(2)
    q = q_ref[0]
    masks = _head_masks()
    n_full = qi * (tq // tk)
    rows = qi * tq + lax.broadcasted_iota(I32, (tq, tk), 0)
    col0 = lax.broadcasted_iota(I32, (tq, tk), 1)
    outs = []
    for hh in range(2):
        qm = jnp.where(masks[hh], q, jnp.zeros_like(q))
        fmax = jnp.max(ft_ref[0, 0, n_full][hh:hh + 1, :], axis=1, keepdims=True)

        def step(j, carry, masked, qm=qm, fmax=fmax, hh=hh):
            m, l, acc = carry
            start = pl.multiple_of(j * tk, tk)
            k = k_ref[0, pl.ds(start, tk), :]
            v = v_ref[0, pl.ds(start, tk), :]
            s = lax.dot_general(qm, k, _NT, preferred_element_type=F32)
            s = s + (fmax - ft_ref[0, 0, j][hh:hh + 1, :])
            if masked:
                s = jnp.where(rows >= j * tk + col0, s, NEG)
            m_new = jnp.maximum(m, jnp.max(s, axis=1, keepdims=True))
            alpha = jnp.exp(m - m_new)
            p = jnp.exp(s - m_new)
            l = alpha * l + jnp.sum(p, axis=1, keepdims=True)
            acc = alpha * acc + jnp.dot(p.astype(BF16), v, preferred_element_type=F32)
            return m_new, l, acc

        carry = (jnp.full((tq, 1), NEG, F32), jnp.zeros((tq, 1), F32), jnp.zeros((tq, LANES), F32))
        carry = lax.fori_loop(0, n_full, functools.partial(step, masked=False), carry)
        for d in range(tq // tk):
            carry = step(n_full + d, carry, True)
        _, l, acc = carry
        outs.append(acc / l)
    y = jnp.where(masks[0], outs[0], outs[1])
    o_ref[0] = (y * _silu(g_ref[0].astype(F32))).astype(o_ref.dtype)


def _fox(zm, ft5, tq, tk):
    B, S, _ = zm.shape
    col = lambda base: (lambda b, h, i: (b, i, base * 4 + h))
    seq = lambda base: (lambda b, h, i: (b, 0, base * 4 + h))
    return pl.pallas_call(
        functools.partial(_fox_kernel, tq=tq, tk=tk),
        grid=(B, 4, S // tq),
        in_specs=[pl.BlockSpec((1, tq, LANES), col(_FQ)),
                  pl.BlockSpec((1, S, LANES), seq(_FK)),
                  pl.BlockSpec((1, S, LANES), seq(_FV)),
                  pl.BlockSpec((1, tq, LANES), col(_FG)),
                  pl.BlockSpec((1, 1, S // tk, 2, tk), lambda b, h, i: (b, h, 0, 0, 0))],
        out_specs=pl.BlockSpec((1, tq, LANES), lambda b, h, i: (b, i, h)),
        out_shape=jax.ShapeDtypeStruct((B, S, WIDTH), BF16),
        compiler_params=_params("arbitrary", "arbitrary", "arbitrary"),
        name="fox_attention",
    )(zm, zm, zm, zm, ft5)


def _sb_kernel(q_ref, k_ref, v_ref, g_ref, o_ref, *, tq):
    tk = LANES
    qi = pl.program_id(2)
    q = q_ref[0]
    masks = _head_masks()
    n_off = qi * (tq // tk)
    rr = lax.broadcasted_iota(I32, (2 * tk, 2 * tk), 0) & (tk - 1)
    cc = lax.broadcasted_iota(I32, (2 * tk, 2 * tk), 1)
    suffix = jnp.where((cc >= tk) | (rr >= cc), 1.0, 0.0).astype(BF16)
    rows = qi * tq + lax.broadcasted_iota(I32, (tq, tk), 0)
    col0 = lax.broadcasted_iota(I32, (tq, tk), 1)
    outs = []
    for hh in range(2):
        qm = jnp.where(masks[hh], q, jnp.zeros_like(q))

        def step(c, carry, masked, qm=qm):
            later, acc = carry
            start = pl.multiple_of(c * tk, tk)
            k = k_ref[0, pl.ds(start, tk), :]
            v = v_ref[0, pl.ds(start, tk), :]
            z = lax.dot_general(qm, k, _NT, preferred_element_type=F32)
            log1m = -(jnp.maximum(z, 0.0) + jnp.log1p(jnp.exp(-jnp.abs(z))))
            if masked:
                valid = rows > c * tk + col0
                log1m = jnp.where(valid, log1m, 0.0)
            hi = log1m.astype(BF16)
            lo = (log1m - hi.astype(F32)).astype(BF16)
            res = jnp.dot(jnp.concatenate([hi, lo], axis=1), suffix, preferred_element_type=F32)
            a = jnp.exp(z + res[:, :tk] + later)
            if masked:
                a = jnp.where(valid, a, 0.0)
            acc = acc + jnp.dot(a.astype(BF16), v, preferred_element_type=F32)
            return later + res[:, tk:], acc

        carry = (jnp.zeros((tq, tk), F32), jnp.zeros((tq, LANES), F32))
        for d in reversed(range(tq // tk)):
            carry = step(n_off + d, carry, True)
        carry = lax.fori_loop(0, n_off, lambda i, cr: step(n_off - 1 - i, cr, False), carry)
        outs.append(carry[1])
    y = jnp.where(masks[0], outs[0], outs[1])
    o_ref[0] = (y * _silu(g_ref[0].astype(F32))).astype(o_ref.dtype)


def _sb(zm, tq):
    B, S, _ = zm.shape
    col = lambda base: (lambda b, h, i: (b, i, base * 4 + h))
    seq = lambda base: (lambda b, h, i: (b, 0, base * 4 + h))
    return pl.pallas_call(
        functools.partial(_sb_kernel, tq=tq),
        grid=(B, 4, S // tq),
        in_specs=[pl.BlockSpec((1, tq, LANES), col(_SQ)),
                  pl.BlockSpec((1, S, LANES), seq(_SK)),
                  pl.BlockSpec((1, S, LANES), seq(_SV)),
                  pl.BlockSpec((1, tq, LANES), col(_SG))],
        out_specs=pl.BlockSpec((1, tq, LANES), lambda b, h, i: (b, i, h)),
        out_shape=jax.ShapeDtypeStruct((B, S, WIDTH), BF16),
        compiler_params=_params("arbitrary", "arbitrary", "arbitrary"),
        name="stick_breaking_attention",
    )(zm, zm, zm, zm)


def _dsa_kernel(iq_ref, iqr_ref, misc_ref, cos_ref, sin_ref, dq_ref, dqr_ref, dg_ref,
                ikk_ref, kk_ref, vv_ref, o_ref, key_ref, bias_ref, cut_ref, *, tq, tk, ktop, seq_bits):
    qi = pl.program_id(1)
    q0 = qi * tq
    nch = (q0 + tq + tk - 1) // tk
    masks = _head_masks()
    cos = cos_ref[0]
    sin = sin_ref[0]
    cos4 = jnp.concatenate([cos] * 4, axis=1)
    sin4 = jnp.concatenate([sin] * 4, axis=1)
    iq = (iq_ref[0] * cos4 + iqr_ref[0] * sin4).astype(BF16)
    qd = (dq_ref[0].astype(F32) * cos4 + dqr_ref[0].astype(F32) * sin4).astype(BF16)
    misc = misc_ref[0]
    rows = q0 + lax.broadcasted_iota(I32, (tq, tk), 0)
    col0 = lax.broadcasted_iota(I32, (tq, tk), 1)

    def head_slab(x, h):
        blk = x[:, (h // 2) * LANES:(h // 2 + 1) * LANES]
        return jnp.where(masks[h % 2], blk, jnp.zeros_like(blk))

    iq_h = [head_slab(iq, h) for h in range(N_HEADS)]
    w_h = [misc[:, N_HEADS + h:N_HEADS + h + 1] * (N_HEADS ** -0.5) for h in range(N_HEADS)]

    def score_chunk(c, carry):
        ik = ikk_ref[0, pl.ds(pl.multiple_of(c * tk, tk), tk), :]
        acc = jnp.zeros((tq, tk), F32)
        for h in range(N_HEADS):
            s = lax.dot_general(iq_h[h], ik, _NT, preferred_element_type=F32)
            acc = acc + jnp.maximum(s, 0.0) * w_h[h]
        acc = jnp.where(acc == 0.0, 0.0, acc)
        bits = lax.bitcast_convert_type(acc, I32)
        key = bits ^ (lax.shift_right_arithmetic(bits, 31) & 0x7FFFFFFF)
        key = jnp.maximum(key, INT_MIN + 1)
        key_ref[c] = jnp.where(c * tk + col0 <= rows, key, INT_MIN)
        return carry

    lax.fori_loop(0, nch, score_chunk, 0)

    def count(pred):
        def body(c, cnt):
            return cnt + jnp.where(pred(key_ref[c], c), 1.0, 0.0)
        cnt = lax.fori_loop(0, nch, body, jnp.zeros((tq, tk), F32))
        return jnp.sum(cnt, axis=1, keepdims=True)

    def bit_body(b, prefix):
        cand_u = prefix | lax.shift_left(jnp.int32(1), 31 - b)
        cand = jnp.broadcast_to(cand_u ^ INT_MIN, (tq, tk))
        n = count(lambda key, c: key >= cand)
        return jnp.where(n >= ktop, cand_u, prefix)

    prefix = lax.fori_loop(0, 32, bit_body, jnp.zeros((tq, 1), I32))
    thr = jnp.broadcast_to(prefix ^ INT_MIN, (tq, tk))
    n_gt = count(lambda key, c: key > thr)
    n_ge = count(lambda key, c: key >= thr)
    need = ktop - n_gt

    cut_ref[...] = jnp.full(cut_ref.shape, 2 ** seq_bits, I32)

    @pl.when(jnp.max(n_ge) > ktop)
    def _():
        def idx_body(b, lo):
            cand_i = lo | lax.shift_left(jnp.int32(1), seq_bits - 1 - b)
            cand = jnp.broadcast_to(cand_i, (tq, tk))
            g = count(lambda key, c: (key == thr) & (c * tk + col0 < cand))
            return jnp.where(g < need, cand_i, lo)
        lo = lax.fori_loop(0, seq_bits, idx_body, jnp.zeros((tq, 1), I32))
        cut_ref[...] = jnp.broadcast_to(lo, cut_ref.shape)

    cut = jnp.broadcast_to(cut_ref[:, :1], (tq, tk))

    def bias_chunk(c, carry):
        key = key_ref[c]
        sel = (key > thr) | ((key == thr) & (c * tk + col0 <= cut))
        sel = sel & (key > INT_MIN)
        bias_ref[c] = jnp.where(sel, 0.0, NEG)
        return carry

    lax.fori_loop(0, nch, bias_chunk, 0)

    for p in range(N_HEADS // 2):
        outs = []
        for hh in range(2):
            qm = head_slab(qd, 2 * p + hh)

            def att(c, carry, qm=qm):
                m, l, acc = carry
                start = pl.multiple_of(c * tk, tk)
                k = kk_ref[0, pl.ds(start, tk), :]
                v = vv_ref[0, pl.ds(start, tk), :]
                s = lax.dot_general(qm, k, _NT, preferred_element_type=F32) + bias_ref[c]
                m_new = jnp.maximum(m, jnp.max(s, axis=1, keepdims=True))
                alpha = jnp.exp(m - m_new)
                pr = jnp.exp(s - m_new)
                l = alpha * l + jnp.sum(pr, axis=1, keepdims=True)
                acc = alpha * acc + jnp.dot(pr.astype(BF16), v, preferred_element_type=F32)
                return m_new, l, acc

            carry = (jnp.full((tq, 1), NEG, F32), jnp.zeros((tq, 1), F32), jnp.zeros((tq, LANES), F32))
            _, l, acc = lax.fori_loop(0, nch, att, carry)
            outs.append(acc / l)
        y = jnp.where(masks[0], outs[0], outs[1])
        g = dg_ref[0, :, p * LANES:(p + 1) * LANES].astype(F32)
        o_ref[0, :, p * LANES:(p + 1) * LANES] = (y * _silu(g)).astype(o_ref.dtype)


def _dsa(zs, zm, cos, sin, ikk, kk, vv, tq, tk, ktop):
    B, S, _ = zs.shape
    wide = lambda col: pl.BlockSpec((1, tq, WIDTH), lambda b, i: (b, i, col))
    row = pl.BlockSpec((1, tq, LANES), lambda b, i: (b, i, 0))
    seq = pl.BlockSpec((1, S, LANES), lambda b, i: (b, 0, 0))
    seq_bits = max(1, (S - 1).bit_length())
    return pl.pallas_call(
        functools.partial(_dsa_kernel, tq=tq, tk=tk, ktop=ktop, seq_bits=seq_bits),
        grid=(B, S // tq),
        in_specs=[wide(_IQ), wide(_IQR),
                  pl.BlockSpec((1, tq, LANES), lambda b, i: (b, i, _MISC)),
                  row, row, wide(_DQ), wide(_DQR), wide(_DG), seq, seq, seq],
        out_specs=pl.BlockSpec((1, tq, WIDTH), lambda b, i: (b, i, 0)),
        out_shape=jax.ShapeDtypeStruct((B, S, WIDTH), BF16),
        scratch_shapes=[pltpu.VMEM((S // tk, tq, tk), I32),
                        pltpu.VMEM((S // tk, tq, tk), F32),
                        pltpu.VMEM((tq, LANES), I32)],
        compiler_params=_params("arbitrary", "arbitrary"),
        name="dsa_attention",
    )(zs, zs, zs, cos, sin, zm, zm, zm, ikk, kk, vv)


def _merge_kernel(x_ref, gate_ref, yf_ref, yd_ref, ys_ref, m0_ref, m1_ref, m2_ref,
                  wf_ref, wd_ref, ws_ref, wo_ref, gfin_ref, o_ref, *, final):
    def branch(y_ref, w_ref, m_ref):
        proj = jnp.dot(y_ref[...], w_ref[...], preferred_element_type=F32)
        return jax.nn.sigmoid(m_ref[...].astype(F32)) * proj

    mixed = branch(yf_ref, wf_ref, m0_ref) + branch(yd_ref, wd_ref, m1_ref) + branch(ys_ref, ws_ref, m2_ref)
    o = x_ref[...] + gate_ref[0] * jnp.dot(mixed.astype(BF16), wo_ref[...], preferred_element_type=F32)
    if final:
        o = o * lax.rsqrt(jnp.mean(o * o, axis=-1, keepdims=True) + EPS) * gfin_ref[...]
    o_ref[...] = o


def _merge(x2, gate, yf, yd, ys, zm2, wf, wd, ws, wo, gfin, S, tm, final):
    M, D = x2.shape
    per_b = S // tm
    ysp = pl.BlockSpec((tm, WIDTH), lambda m: (m, 0))
    msp = lambda j: pl.BlockSpec((tm, D), lambda m: (m, j))
    wsp = pl.BlockSpec((WIDTH, D), lambda m: (0, 0))
    return pl.pallas_call(
        functools.partial(_merge_kernel, final=final),
        grid=(M // tm,),
        in_specs=[pl.BlockSpec((tm, D), lambda m: (m, 0)),
                  pl.BlockSpec((1, 1, D), lambda m: (m // per_b, 0, 0)),
                  ysp, ysp, ysp, msp(0), msp(1), msp(2), wsp, wsp, wsp,
                  pl.BlockSpec((D, D), lambda m: (0, 0)),
                  pl.BlockSpec((1, D), lambda m: (0, 0))],
        out_specs=pl.BlockSpec((tm, D), lambda m: (m, 0)),
        out_shape=jax.ShapeDtypeStruct((M, D), F32),
        compiler_params=_params("arbitrary"),
        name="merge_out",
    )(x2, gate, yf, yd, ys, zm2, zm2, zm2, wf, wd, ws, wo, gfin)


def _rot_cols(w):
    d, n = w.shape
    w4 = w.reshape(d, n // HEAD_DIM, 2, HEAD_DIM // 2)
    return jnp.concatenate([-w4[:, :, 1:], w4[:, :, :1]], axis=2).reshape(d, n)


def _split_w_in(w):
    sizes = (WIDTH, WIDTH, WIDTH, N_HEADS, WIDTH,
             WIDTH, KV_RANK, WIDTH, HEAD_DIM, N_HEADS, WIDTH,
             WIDTH, WIDTH, WIDTH, WIDTH, w.shape[1] - (11 * WIDTH + 2 * N_HEADS + KV_RANK + HEAD_DIM))
    parts, off = [], 0
    for s in sizes:
        parts.append(w[:, off:off + s])
        off += s
    return parts


def _layer_weights(w_in, w_kv_up):
    (fq, fk, fv, ff, fg, dq, dckv, diq, dik, diw, dg, sq, sk, sv, sg, merge) = _split_w_in(w_in)
    scale = HEAD_DIM ** -0.5
    fq, dq, sq, diq = fq * scale, dq * scale, sq * scale, diq * scale
    w_main = jnp.concatenate([merge, fq, fk, fv, fg, dq, _rot_cols(dq), dg, sq, sk, sv, sg], axis=1)
    pad = jnp.zeros((w_in.shape[0], LANES - 2 * N_HEADS), w_in.dtype)
    dikr = _rot_cols(dik)
    w_small = jnp.concatenate([diq, _rot_cols(diq), ff, diw, pad, dckv, dik, dik, dikr, dikr], axis=1)
    wk, wv = w_kv_up[:, :HEAD_DIM], w_kv_up[:, HEAD_DIM:]
    wkr = _rot_cols(wk)
    w_kv = jnp.concatenate([wk, wk, wkr, wkr, wv, wv], axis=1)
    return w_main.astype(BF16), w_small.astype(BF16), w_kv.astype(BF16)


def kernel(x, c, positions, w_ada, b_ada, g_norm, w_in, b_fgt, g_kv, w_kv_up, w_br_fox, w_br_dsa, w_br_sb, w_out, g_final):
    B, S, D = x.shape
    depth = w_in.shape[0]
    assert D == 2 * WIDTH and S % 256 == 0
    ktop = min(TOPK_MAX, S // 4)
    tm = min(1024, S)
    tp = min(512, S)
    tq_att = 256
    tk_fox = 256

    half = HEAD_DIM // 2
    inv_freq = ROPE_THETA ** (-jnp.arange(half, dtype=F32) / half)
    ang = positions.astype(F32)[..., None] * inv_freq
    cos = jnp.tile(jnp.cos(ang), (1, 1, LANES // half))
    sin = jnp.tile(jnp.sin(ang), (1, 1, LANES // half))

    mod = _modulation(c, w_ada, b_ada)
    x2 = x.reshape(B * S, D)
    for l in range(depth):
        shift = mod[l, :, None, :D]
        scale = mod[l, :, None, D:2 * D]
        gate = mod[l, :, None, 2 * D:]
        w_main, w_small, w_kv = _layer_weights(w_in[l], w_kv_up[l])
        g = g_norm[l].reshape(1, D)
        zm2 = _norm_proj(x2, g, scale, shift, w_main, S, BF16, tm, 512, "norm_proj_main")
        zs2 = _norm_proj(x2, g, scale, shift, w_small, S, F32, tm, 512, "norm_proj_small")
        zm = zm2.reshape(B, S, N_MAIN)
        zs = zs2.reshape(B, S, N_SMALL)
        bf = jnp.zeros((1, LANES), F32).at[0, :N_HEADS].set(b_fgt[l])
        ft, kk, vv, ikk = _prep(zs, cos, sin, bf, g_kv[l].reshape(1, KV_RANK), w_kv, tp)
        ft5 = ft.reshape(B, 4, 2, S // tk_fox, tk_fox).transpose(0, 1, 3, 2, 4)
        y_fox = _fox(zm, ft5, tq_att, tk_fox)
        y_sb = _sb(zm, tq_att)
        y_dsa = _dsa(zs, zm, cos, sin, ikk, kk, vv, 128, 256, ktop)
        x2 = _merge(x2, gate, y_fox.reshape(B * S, WIDTH), y_dsa.reshape(B * S, WIDTH),
                    y_sb.reshape(B * S, WIDTH), zm2,
                    w_br_fox[l].astype(BF16), w_br_dsa[l].astype(BF16), w_br_sb[l].astype(BF16),
                    w_out[l].astype(BF16), g_final.reshape(1, D), S, min(512, S), l == depth - 1)
    return x2.reshape(B, S, D)
```

```python
import functools

import jax
import jax.numpy as jnp
from jax import lax
from jax.experimental import pallas as pl
from jax.experimental.pallas import tpu as pltpu

F32 = jnp.float32
BF16 = jnp.bfloat16
I32 = jnp.int32
HIGHEST = lax.Precision.HIGHEST

HEAD_DIM = 64
N_HEADS = 8
WIDTH = N_HEADS * HEAD_DIM
KV_RANK = 128
TOPK_MAX = 256
ROPE_THETA = 10000.0
EPS = 1e-6
LANES = 128
NEG = -1e30
INT_MIN = -(2 ** 31)
VMEM_LIMIT = 56 * 1024 * 1024

_MERGE, _FQ, _FK, _FV, _FG, _DQ, _DQR, _DG, _SQ, _SK, _SV, _SG = 0, 6, 7, 8, 9, 10, 11, 12, 13, 14, 15, 16
N_MAIN = 17 * WIDTH
_IQ, _IQR = 0, 1
_MISC, _CKV, _IK, _IKR = 8, 9, 10, 11
N_SMALL = 12 * LANES

_NT = (((1,), (1,)), ((), ()))


def _params(*sem):
    return pltpu.CompilerParams(dimension_semantics=sem, vmem_limit_bytes=VMEM_LIMIT)


def _mod_kernel(c_ref, w_ref, b_ref, o_ref):
    c = c_ref[...]
    sc = c * jax.nn.sigmoid(c)
    o_ref[0] = jnp.dot(sc, w_ref[0], preferred_element_type=F32, precision=HIGHEST) + b_ref[0]


def _modulation(c, w_ada, b_ada):
    L, D, N = w_ada.shape
    B = c.shape[0]
    tn = 512
    return pl.pallas_call(
        _mod_kernel,
        grid=(L, N // tn),
        in_specs=[pl.BlockSpec((B, D), lambda l, n: (0, 0)),
                  pl.BlockSpec((1, D, tn), lambda l, n: (l, 0, n)),
                  pl.BlockSpec((1, 1, tn), lambda l, n: (l, 0, n))],
        out_specs=pl.BlockSpec((1, B, tn), lambda l, n: (l, 0, n)),
        out_shape=jax.ShapeDtypeStruct((L, B, N), F32),
        compiler_params=_params("arbitrary", "arbitrary"),
        name="adaln_mod",
    )(c, w_ada, b_ada.reshape(L, 1, N))


def _norm_proj_kernel(x_ref, g_ref, sc_ref, sh_ref, w_ref, o_ref, h_ref):
    @pl.when(pl.program_id(1) == 0)
    def _():
        x = x_ref[...]
        y = x * lax.rsqrt(jnp.mean(x * x, axis=-1, keepdims=True) + EPS)
        h = (y * g_ref[...]) * (1.0 + sc_ref[0]) + sh_ref[0]
        h_ref[...] = h.astype(BF16)

    o_ref[...] = jnp.dot(h_ref[...], w_ref[...], preferred_element_type=F32).astype(o_ref.dtype)


def _norm_proj(x2, g, scale, shift, w, S, out_dtype, tm, tn, name):
    M, D = x2.shape
    N = w.shape[1]
    per_b = S // tm
    return pl.pallas_call(
        _norm_proj_kernel,
        grid=(M // tm, N // tn),
        in_specs=[pl.BlockSpec((tm, D), lambda m, n: (m, 0)),
                  pl.BlockSpec((1, D), lambda m, n: (0, 0)),
                  pl.BlockSpec((1, 1, D), lambda m, n: (m // per_b, 0, 0)),
                  pl.BlockSpec((1, 1, D), lambda m, n: (m // per_b, 0, 0)),
                  pl.BlockSpec((D, tn), lambda m, n: (0, n))],
        out_specs=pl.BlockSpec((tm, tn), lambda m, n: (m, n)),
        out_shape=jax.ShapeDtypeStruct((M, N), out_dtype),
        scratch_shapes=[pltpu.VMEM((tm, D), BF16)],
        compiler_params=_params("arbitrary", "arbitrary"),
        name=name,
    )(x2, g, scale, shift, w)


def _prep_kernel(misc_ref, ckv_ref, ik_ref, ikr_ref, cos_ref, sin_ref, bf_ref, gkv_ref, wkv_ref,
                 ft_ref, kk_ref, vv_ref, ikk_ref, carry_ref):
    @pl.when(pl.program_id(1) == 0)
    def _():
        carry_ref[...] = jnp.zeros_like(carry_ref)

    tp = misc_ref.shape[1]
    xf = misc_ref[0] + bf_ref[...]
    logf = jnp.minimum(xf, 0.0) - jnp.log1p(jnp.exp(-jnp.abs(xf)))
    r = lax.broadcasted_iota(I32, (tp, tp), 0)
    c = lax.broadcasted_iota(I32, (tp, tp), 1)
    tri = jnp.where(r >= c, 1.0, 0.0).astype(F32)
    cum = jnp.dot(tri, logf, preferred_element_type=F32, precision=HIGHEST) + carry_ref[...]
    carry_ref[...] = cum[tp - 1:tp, :]
    ft_ref[0] = cum.T[:N_HEADS, :]

    cos = cos_ref[0]
    sin = sin_ref[0]
    ckv = ckv_ref[0]
    nrm = ckv * lax.rsqrt(jnp.mean(ckv * ckv, axis=-1, keepdims=True) + EPS) * gkv_ref[...]
    kv = jnp.dot(nrm.astype(BF16), wkv_ref[...], preferred_element_type=F32)
    kk_ref[0] = (kv[:, :LANES] * cos + kv[:, LANES:2 * LANES] * sin).astype(BF16)
    vv_ref[0] = kv[:, 2 * LANES:].astype(BF16)
    ikk_ref[0] = (ik_ref[0] * cos + ikr_ref[0] * sin).astype(BF16)


def _prep(zs, cos, sin, bf, gkv, wkv, tp):
    B, S, _ = zs.shape
    blk = lambda col: pl.BlockSpec((1, tp, LANES), lambda b, j: (b, j, col))
    row = pl.BlockSpec((1, tp, LANES), lambda b, j: (b, j, 0))
    return pl.pallas_call(
        _prep_kernel,
        grid=(B, S // tp),
        in_specs=[blk(_MISC), blk(_CKV), blk(_IK), blk(_IKR), row, row,
                  pl.BlockSpec((1, LANES), lambda b, j: (0, 0)),
                  pl.BlockSpec((1, LANES), lambda b, j: (0, 0)),
                  pl.BlockSpec((KV_RANK, 3 * LANES), lambda b, j: (0, 0))],
        out_specs=[pl.BlockSpec((1, N_HEADS, tp), lambda b, j: (b, 0, j)),
                   row, row, row],
        out_shape=[jax.ShapeDtypeStruct((B, N_HEADS, S), F32),
                   jax.ShapeDtypeStruct((B, S, LANES), BF16),
                   jax.ShapeDtypeStruct((B, S, LANES), BF16),
                   jax.ShapeDtypeStruct((B, S, LANES), BF16)],
        scratch_shapes=[pltpu.VMEM((1, LANES), F32)],
        compiler_params=_params("arbitrary", "arbitrary"),
        name="prep",
    )(zs, zs, zs, zs, cos, sin, bf, gkv, wkv)


def _head_masks():
    lane = lax.broadcasted_iota(I32, (1, LANES), 1)
    lo = lane < HEAD_DIM
    return lo, jnp.logical_not(lo)


def _silu(g):
    return g * jax.nn.sigmoid(g)


def _pair_heads(q, masks):
    slabs = []
    for p in range(q.shape[1] // LANES):
        blk = q[:, p * LANES:(p + 1) * LANES]
        slabs += [jnp.where(masks[hh], blk, jnp.zeros_like(blk)) for hh in range(2)]
    return slabs


def _join_heads(outs, masks):
    return jnp.concatenate([jnp.where(masks[0], outs[2 * p], outs[2 * p + 1]) for p in range(len(outs) // 2)], axis=1)


def _fox_kernel(q_ref, k_ref, v_ref, g_ref, ft_ref, o_ref, *, tq, tk):
    qi = pl.program_id(2)
    masks = _head_masks()
    assert tq % tk == 0 or tk % tq == 0
    n_full = (qi * tq) // tk
    rows = qi * tq + lax.broadcasted_iota(I32, (tq, tk), 0)
    col0 = lax.broadcasted_iota(I32, (tq, tk), 1)
    qm = _pair_heads(q_ref[0], masks)
    nh = len(qm)
    f0 = jnp.max(ft_ref[0, 0, n_full], axis=1, keepdims=True)

    def step(j, carry, masked):
        start = pl.multiple_of(j * tk, tk)
        k = k_ref[0, pl.ds(start, tk), :]
        v = v_ref[0, pl.ds(start, tk), :]
        brow = f0 - ft_ref[0, 0, j]
        out = []
        for h in range(nh):
            m, l, acc = carry[h]
            lanes = slice((h // 2) * LANES, (h // 2 + 1) * LANES)
            s = lax.dot_general(qm[h], k[:, lanes], _NT, preferred_element_type=F32) + brow[h:h + 1, :]
            if masked:
                s = jnp.where(rows >= j * tk + col0, s, NEG)
            m_new = jnp.maximum(m, jnp.max(s, axis=1, keepdims=True))
            alpha = jnp.exp(m - m_new)
            p = jnp.exp(s - m_new)
            l = alpha * l + jnp.sum(p, axis=1, keepdims=True)
            acc = alpha * acc + jnp.dot(p.astype(BF16), v[:, lanes], preferred_element_type=F32)
            out.append((m_new, l, acc))
        return tuple(out)

    init = (jnp.full((tq, 1), NEG, F32), jnp.zeros((tq, 1), F32), jnp.zeros((tq, LANES), F32))
    carry = lax.fori_loop(0, n_full, functools.partial(step, masked=False), (init,) * nh)
    for d in range(max(1, tq // tk)):
        carry = step(n_full + d, carry, True)
    y = _join_heads([acc / l for _, l, acc in carry], masks)
    o_ref[0] = (y * _silu(g_ref[0].astype(F32))).astype(o_ref.dtype)


def _fox(zm, ft, tq, tk, npair):
    B, S, _ = zm.shape
    w = npair * LANES
    ng = 4 // npair
    col = lambda base: (lambda b, h, i: (b, i, base * ng + h))
    seq = lambda base: (lambda b, h, i: (b, 0, base * ng + h))
    ft5 = ft.reshape(B, ng, 2 * npair, S // tk, tk).transpose(0, 1, 3, 2, 4)
    return pl.pallas_call(
        functools.partial(_fox_kernel, tq=tq, tk=tk),
        grid=(B, ng, S // tq),
        in_specs=[pl.BlockSpec((1, tq, w), col(_FQ)),
                  pl.BlockSpec((1, S, w), seq(_FK)),
                  pl.BlockSpec((1, S, w), seq(_FV)),
                  pl.BlockSpec((1, tq, w), col(_FG)),
                  pl.BlockSpec((1, 1, S // tk, 2 * npair, tk), lambda b, h, i: (b, h, 0, 0, 0))],
        out_specs=pl.BlockSpec((1, tq, w), lambda b, h, i: (b, i, h)),
        out_shape=jax.ShapeDtypeStruct((B, S, WIDTH), BF16),
        compiler_params=_params("arbitrary", "arbitrary", "arbitrary"),
        name="fox_attention",
    )(zm, zm, zm, zm, ft5)


def _sb_kernel(q_ref, k_ref, v_ref, g_ref, sfx_ref, o_ref, *, tq, tk):
    qi = pl.program_id(2)
    masks = _head_masks()
    qm = _pair_heads(q_ref[0], masks)
    nh = len(qm)
    n_off = qi * (tq // tk)
    rows = lax.broadcasted_iota(I32, (tq, tk), 0)
    col0 = lax.broadcasted_iota(I32, (tq, tk), 1)

    def step(c, carry, masked):
        start = pl.multiple_of(c * tk, tk)
        k = k_ref[0, pl.ds(start, tk), :]
        v = v_ref[0, pl.ds(start, tk), :]
        if masked:
            valid = rows > (c - n_off) * tk + col0
        out = []
        for h in range(nh):
            later, acc = carry[h]
            lanes = slice((h // 2) * LANES, (h // 2 + 1) * LANES)
            z = lax.dot_general(qm[h], k[:, lanes], _NT, preferred_element_type=F32)
            log1m = -(jnp.maximum(z, 0.0) + jnp.log(1.0 + jnp.exp(-jnp.abs(z))))
            if masked:
                log1m = jnp.where(valid, log1m, 0.0)
            hi = log1m.astype(BF16)
            lo = (log1m - hi.astype(F32)).astype(BF16)
            incl = jnp.dot(jnp.concatenate([hi, lo], axis=1), sfx_ref[...], preferred_element_type=F32)
            a = jnp.exp(z + incl + later)
            if masked:
                a = jnp.where(valid, a, 0.0)
            acc = acc + jnp.dot(a.astype(BF16), v[:, lanes], preferred_element_type=F32)
            out.append((later + incl[:, :1], acc))
        return tuple(out)

    carry = ((jnp.zeros((tq, 1), F32), jnp.zeros((tq, LANES), F32)),) * nh
    for d in reversed(range(tq // tk)):
        carry = step(n_off + d, carry, True)
    carry = lax.fori_loop(0, n_off, lambda i, cr: step(n_off - 1 - i, cr, False), carry)
    y = _join_heads([acc for _, acc in carry], masks)
    o_ref[0] = (y * _silu(g_ref[0].astype(F32))).astype(o_ref.dtype)


def _sb(zm, tq, tk, npair):
    B, S, _ = zm.shape
    assert tq % tk == 0
    w = npair * LANES
    ng = 4 // npair
    col = lambda base: (lambda b, h, i: (b, i, base * ng + h))
    seq = lambda base: (lambda b, h, i: (b, 0, base * ng + h))
    j = jnp.arange(2 * tk, dtype=I32)[:, None] % tk
    sfx = (j >= jnp.arange(tk, dtype=I32)[None, :]).astype(BF16)
    return pl.pallas_call(
        functools.partial(_sb_kernel, tq=tq, tk=tk),
        grid=(B, ng, S // tq),
        in_specs=[pl.BlockSpec((1, tq, w), col(_SQ)),
                  pl.BlockSpec((1, S, w), seq(_SK)),
                  pl.BlockSpec((1, S, w), seq(_SV)),
                  pl.BlockSpec((1, tq, w), col(_SG)),
                  pl.BlockSpec((2 * tk, tk), lambda b, h, i: (0, 0))],
        out_specs=pl.BlockSpec((1, tq, w), lambda b, h, i: (b, i, h)),
        out_shape=jax.ShapeDtypeStruct((B, S, WIDTH), BF16),
        compiler_params=_params("arbitrary", "arbitrary", "arbitrary"),
        name="stick_breaking_attention",
    )(zm, zm, zm, zm, sfx)


def _dsa_kernel(iq_ref, iqr_ref, misc_ref, cos_ref, sin_ref, dq_ref, dqr_ref, dg_ref,
                ikk_ref, kk_ref, vv_ref, o_ref, key_ref, bias_ref, cut_ref, *, tq, tk, ktop, seq_bits):
    qi = pl.program_id(1)
    q0 = qi * tq
    nch = (q0 + tq + tk - 1) // tk
    masks = _head_masks()
    cos = cos_ref[0]
    sin = sin_ref[0]
    cos4 = jnp.concatenate([cos] * 4, axis=1)
    sin4 = jnp.concatenate([sin] * 4, axis=1)
    iq = (iq_ref[0] * cos4 + iqr_ref[0] * sin4).astype(BF16)
    qd = (dq_ref[0].astype(F32) * cos4 + dqr_ref[0].astype(F32) * sin4).astype(BF16)
    misc = misc_ref[0]
    rows = q0 + lax.broadcasted_iota(I32, (tq, tk), 0)
    col0 = lax.broadcasted_iota(I32, (tq, tk), 1)

    def head_slab(x, h):
        blk = x[:, (h // 2) * LANES:(h // 2 + 1) * LANES]
        return jnp.where(masks[h % 2], blk, jnp.zeros_like(blk))

    iq_h = [head_slab(iq, h) for h in range(N_HEADS)]
    w_h = [misc[:, N_HEADS + h:N_HEADS + h + 1] * (N_HEADS ** -0.5) for h in range(N_HEADS)]

    def score_chunk(c, carry):
        ik = ikk_ref[0, pl.ds(pl.multiple_of(c * tk, tk), tk), :]
        acc = jnp.zeros((tq, tk), F32)
        for h in range(N_HEADS):
            s = lax.dot_general(iq_h[h], ik, _NT, preferred_element_type=F32)
            acc = acc + jnp.maximum(s, 0.0) * w_h[h]
        acc = jnp.where(acc == 0.0, 0.0, acc)
        bits = lax.bitcast_convert_type(acc, I32)
        key = bits ^ (lax.shift_right_arithmetic(bits, 31) & 0x7FFFFFFF)
        key = jnp.maximum(key, INT_MIN + 1)
        key_ref[c] = jnp.where(c * tk + col0 <= rows, key, INT_MIN)
        return carry

    lax.fori_loop(0, nch, score_chunk, 0)

    lane_col = lax.broadcasted_iota(I32, (tq, LANES), 1)

    def count(pred):
        def body(c, cnt):
            key = key_ref[c]
            for half in range(tk // LANES):
                col = c * tk + half * LANES + lane_col
                cnt = cnt + jnp.where(pred(key[:, half * LANES:(half + 1) * LANES], col), 1.0, 0.0)
            return cnt
        cnt = lax.fori_loop(0, nch, body, jnp.zeros((tq, LANES), F32))
        return jnp.sum(cnt, axis=1, keepdims=True)

    def bit_body(b, prefix):
        cand_u = prefix | lax.shift_left(jnp.int32(1), 31 - b)
        cand = jnp.broadcast_to(cand_u ^ INT_MIN, (tq, LANES))
        n = count(lambda key, col: key >= cand)
        return jnp.where(n >= ktop, cand_u, prefix)

    prefix = lax.fori_loop(0, 32, bit_body, jnp.zeros((tq, 1), I32))
    thr1 = prefix ^ INT_MIN
    thr_h = jnp.broadcast_to(thr1, (tq, LANES))
    n_gt = count(lambda key, col: key > thr_h)
    n_ge = count(lambda key, col: key >= thr_h)
    need = ktop - n_gt

    cut_ref[...] = jnp.full(cut_ref.shape, 2 ** seq_bits, I32)

    @pl.when(jnp.max(n_ge) > ktop)
    def _():
        def idx_body(b, lo):
            cand_i = lo | lax.shift_left(jnp.int32(1), seq_bits - 1 - b)
            cand = jnp.broadcast_to(cand_i, (tq, LANES))
            g = count(lambda key, col: (key == thr_h) & (col < cand))
            return jnp.where(g < need, cand_i, lo)
        lo = lax.fori_loop(0, seq_bits, idx_body, jnp.zeros((tq, 1), I32))
        cut_ref[...] = jnp.broadcast_to(lo, cut_ref.shape)

    thr = jnp.broadcast_to(thr1, (tq, tk))
    cut = jnp.broadcast_to(cut_ref[:, :1], (tq, tk))

    def bias_chunk(c, carry):
        key = key_ref[c]
        sel = (key > thr) | ((key == thr) & (c * tk + col0 <= cut))
        sel = sel & (key > INT_MIN)
        bias_ref[c] = jnp.where(sel, 0.0, NEG)
        return carry

    lax.fori_loop(0, nch, bias_chunk, 0)

    rows_all = N_HEADS * tq
    qs = jnp.concatenate([head_slab(qd, h) for h in range(N_HEADS)], axis=0)

    def att(c, carry):
        m, l, acc = carry
        start = pl.multiple_of(c * tk, tk)
        k = kk_ref[0, pl.ds(start, tk), :]
        v = vv_ref[0, pl.ds(start, tk), :]
        s = lax.dot_general(qs, k, _NT, preferred_element_type=F32)
        s = (s.reshape(N_HEADS, tq, tk) + bias_ref[c][None]).reshape(rows_all, tk)
        m_new = jnp.maximum(m, jnp.max(s, axis=1, keepdims=True))
        alpha = jnp.exp(m - m_new)
        pr = jnp.exp(s - m_new)
        l = alpha * l + jnp.sum(pr, axis=1, keepdims=True)
        acc = alpha * acc + jnp.dot(pr.astype(BF16), v, preferred_element_type=F32)
        return m_new, l, acc

    init = (jnp.full((rows_all, 1), NEG, F32), jnp.zeros((rows_all, 1), F32), jnp.zeros((rows_all, LANES), F32))
    _, l, acc = lax.fori_loop(0, nch, att, init)
    out = acc / l
    for p in range(N_HEADS // 2):
        y = jnp.where(masks[0], out[2 * p * tq:(2 * p + 1) * tq], out[(2 * p + 1) * tq:(2 * p + 2) * tq])
        g = dg_ref[0, :, p * LANES:(p + 1) * LANES].astype(F32)
        o_ref[0, :, p * LANES:(p + 1) * LANES] = (y * _silu(g)).astype(o_ref.dtype)


def _dsa(zs, zm, cos, sin, ikk, kk, vv, tq, tk, ktop):
    B, S, _ = zs.shape
    wide = lambda col: pl.BlockSpec((1, tq, WIDTH), lambda b, i: (b, i, col))
    row = pl.BlockSpec((1, tq, LANES), lambda b, i: (b, i, 0))
    seq = pl.BlockSpec((1, S, LANES), lambda b, i: (b, 0, 0))
    seq_bits = max(1, (S - 1).bit_length())
    return pl.pallas_call(
        functools.partial(_dsa_kernel, tq=tq, tk=tk, ktop=ktop, seq_bits=seq_bits),
        grid=(B, S // tq),
        in_specs=[wide(_IQ), wide(_IQR),
                  pl.BlockSpec((1, tq, LANES), lambda b, i: (b, i, _MISC)),
                  row, row, wide(_DQ), wide(_DQR), wide(_DG), seq, seq, seq],
        out_specs=pl.BlockSpec((1, tq, WIDTH), lambda b, i: (b, i, 0)),
        out_shape=jax.ShapeDtypeStruct((B, S, WIDTH), BF16),
        scratch_shapes=[pltpu.VMEM((S // tk, tq, tk), I32),
                        pltpu.VMEM((S // tk, tq, tk), F32),
                        pltpu.VMEM((tq, LANES), I32)],
        compiler_params=_params("arbitrary", "arbitrary"),
        name="dsa_attention",
    )(zs, zs, zs, cos, sin, zm, zm, zm, ikk, kk, vv)


def _merge_kernel(x_ref, gate_ref, yf_ref, yd_ref, ys_ref, m0_ref, m1_ref, m2_ref,
                  wf_ref, wd_ref, ws_ref, wo_ref, gfin_ref, o_ref, *, final):
    def branch(y_ref, w_ref, m_ref):
        proj = jnp.dot(y_ref[...], w_ref[...], preferred_element_type=F32)
        return jax.nn.sigmoid(m_ref[...].astype(F32)) * proj

    mixed = branch(yf_ref, wf_ref, m0_ref) + branch(yd_ref, wd_ref, m1_ref) + branch(ys_ref, ws_ref, m2_ref)
    o = x_ref[...] + gate_ref[0] * jnp.dot(mixed.astype(BF16), wo_ref[...], preferred_element_type=F32)
    if final:
        o = o * lax.rsqrt(jnp.mean(o * o, axis=-1, keepdims=True) + EPS) * gfin_ref[...]
    o_ref[...] = o


def _merge(x2, gate, yf, yd, ys, zm2, wf, wd, ws, wo, gfin, S, tm, final):
    M, D = x2.shape
    per_b = S // tm
    ysp = pl.BlockSpec((tm, WIDTH), lambda m: (m, 0))
    msp = lambda j: pl.BlockSpec((tm, D), lambda m: (m, j))
    wsp = pl.BlockSpec((WIDTH, D), lambda m: (0, 0))
    return pl.pallas_call(
        functools.partial(_merge_kernel, final=final),
        grid=(M // tm,),
        in_specs=[pl.BlockSpec((tm, D), lambda m: (m, 0)),
                  pl.BlockSpec((1, 1, D), lambda m: (m // per_b, 0, 0)),
                  ysp, ysp, ysp, msp(0), msp(1), msp(2), wsp, wsp, wsp,
                  pl.BlockSpec((D, D), lambda m: (0, 0)),
                  pl.BlockSpec((1, D), lambda m: (0, 0))],
        out_specs=pl.BlockSpec((tm, D), lambda m: (m, 0)),
        out_shape=jax.ShapeDtypeStruct((M, D), F32),
        compiler_params=_params("arbitrary"),
        name="merge_out",
    )(x2, gate, yf, yd, ys, zm2, zm2, zm2, wf, wd, ws, wo, gfin)


def _rot_cols(w):
    d, n = w.shape
    w4 = w.reshape(d, n // HEAD_DIM, 2, HEAD_DIM // 2)
    return jnp.concatenate([-w4[:, :, 1:], w4[:, :, :1]], axis=2).reshape(d, n)


def _split_w_in(w):
    sizes = (WIDTH, WIDTH, WIDTH, N_HEADS, WIDTH,
             WIDTH, KV_RANK, WIDTH, HEAD_DIM, N_HEADS, WIDTH,
             WIDTH, WIDTH, WIDTH, WIDTH, w.shape[1] - (11 * WIDTH + 2 * N_HEADS + KV_RANK + HEAD_DIM))
    parts, off = [], 0
    for s in sizes:
        parts.append(w[:, off:off + s])
        off += s
    return parts


def _layer_weights(w_in, w_kv_up):
    (fq, fk, fv, ff, fg, dq, dckv, diq, dik, diw, dg, sq, sk, sv, sg, merge) = _split_w_in(w_in)
    scale = HEAD_DIM ** -0.5
    fq, dq, sq, diq = fq * scale, dq * scale, sq * scale, diq * scale
    w_main = jnp.concatenate([merge, fq, fk, fv, fg, dq, _rot_cols(dq), dg, sq, sk, sv, sg], axis=1)
    pad = jnp.zeros((w_in.shape[0], LANES - 2 * N_HEADS), w_in.dtype)
    dikr = _rot_cols(dik)
    w_small = jnp.concatenate([diq, _rot_cols(diq), ff, diw, pad, dckv, dik, dik, dikr, dikr], axis=1)
    wk, wv = w_kv_up[:, :HEAD_DIM], w_kv_up[:, HEAD_DIM:]
    wkr = _rot_cols(wk)
    w_kv = jnp.concatenate([wk, wk, wkr, wkr, wv, wv], axis=1)
    return w_main.astype(BF16), w_small.astype(BF16), w_kv.astype(BF16)


def kernel(x, c, positions, w_ada, b_ada, g_norm, w_in, b_fgt, g_kv, w_kv_up, w_br_fox, w_br_dsa, w_br_sb, w_out, g_final):
    B, S, D = x.shape
    depth = w_in.shape[0]
    assert D == 2 * WIDTH and S % 256 == 0
    ktop = min(TOPK_MAX, S // 4)
    tm = min(1024, S)
    tp = min(512, S)
    tq_att = 256
    tk_fox = min(512, S)

    half = HEAD_DIM // 2
    inv_freq = ROPE_THETA ** (-jnp.arange(half, dtype=F32) / half)
    ang = positions.astype(F32)[..., None] * inv_freq
    cos = jnp.tile(jnp.cos(ang), (1, 1, LANES // half))
    sin = jnp.tile(jnp.sin(ang), (1, 1, LANES // half))

    mod = _modulation(c, w_ada, b_ada)
    x2 = x.reshape(B * S, D)
    for l in range(depth):
        shift = mod[l, :, None, :D]
        scale = mod[l, :, None, D:2 * D]
        gate = mod[l, :, None, 2 * D:]
        w_main, w_small, w_kv = _layer_weights(w_in[l], w_kv_up[l])
        g = g_norm[l].reshape(1, D)
        zm2 = _norm_proj(x2, g, scale, shift, w_main, S, BF16, tm, 512, "norm_proj_main")
        zs2 = _norm_proj(x2, g, scale, shift, w_small, S, F32, tm, 512, "norm_proj_small")
        zm = zm2.reshape(B, S, N_MAIN)
        zs = zs2.reshape(B, S, N_SMALL)
        bf = jnp.zeros((1, LANES), F32).at[0, :N_HEADS].set(b_fgt[l])
        ft, kk, vv, ikk = _prep(zs, cos, sin, bf, g_kv[l].reshape(1, KV_RANK), w_kv, tp)
        y_fox = _fox(zm, ft, min(512, S), tk_fox, 1)
        y_sb = _sb(zm, min(512, S), 256, 1)
        y_dsa = _dsa(zs, zm, cos, sin, ikk, kk, vv, 128, 256, ktop)
        x2 = _merge(x2, gate, y_fox.reshape(B * S, WIDTH), y_dsa.reshape(B * S, WIDTH),
                    y_sb.reshape(B * S, WIDTH), zm2,
                    w_br_fox[l].astype(BF16), w_br_dsa[l].astype(BF16), w_br_sb[l].astype(BF16),
                    w_out[l].astype(BF16), g_final.reshape(1, D), S, min(512, S), l == depth - 1)
    return x2.reshape(B, S, D)
```

```python
import functools

import jax
import jax.numpy as jnp
from jax import lax
from jax.experimental import pallas as pl
from jax.experimental.pallas import tpu as pltpu

F32 = jnp.float32
BF16 = jnp.bfloat16
I32 = jnp.int32
HIGHEST = lax.Precision.HIGHEST

HEAD_DIM = 64
N_HEADS = 8
WIDTH = N_HEADS * HEAD_DIM
KV_RANK = 128
TOPK_MAX = 256
ROPE_THETA = 10000.0
EPS = 1e-6
LANES = 128
NEG = -1e30
INT_MIN = -(2 ** 31)
VMEM_LIMIT = 56 * 1024 * 1024

_MERGE, _FQ, _FK, _FV, _FG, _DQ, _DQR, _DG, _SQ, _SK, _SV, _SG = 0, 6, 7, 8, 9, 10, 11, 12, 13, 14, 15, 16
N_MAIN = 17 * WIDTH
_IQ, _IQR = 0, 1
_MISC, _CKV, _IK, _IKR = 8, 9, 10, 11
N_SMALL = 12 * LANES

_NT = (((1,), (1,)), ((), ()))


def _params(*sem):
    return pltpu.CompilerParams(dimension_semantics=sem, vmem_limit_bytes=VMEM_LIMIT)


def _mod_kernel(c_ref, w_ref, b_ref, o_ref):
    c = c_ref[...]
    sc = c * jax.nn.sigmoid(c)
    o_ref[0] = jnp.dot(sc, w_ref[0], preferred_element_type=F32, precision=HIGHEST) + b_ref[0]


def _modulation(c, w_ada, b_ada):
    L, D, N = w_ada.shape
    B = c.shape[0]
    tn = 512
    return pl.pallas_call(
        _mod_kernel,
        grid=(L, N // tn),
        in_specs=[pl.BlockSpec((B, D), lambda l, n: (0, 0)),
                  pl.BlockSpec((1, D, tn), lambda l, n: (l, 0, n)),
                  pl.BlockSpec((1, 1, tn), lambda l, n: (l, 0, n))],
        out_specs=pl.BlockSpec((1, B, tn), lambda l, n: (l, 0, n)),
        out_shape=jax.ShapeDtypeStruct((L, B, N), F32),
        compiler_params=_params("arbitrary", "arbitrary"),
        name="adaln_mod",
    )(c, w_ada, b_ada.reshape(L, 1, N))


def _norm_proj_kernel(x_ref, g_ref, sc_ref, sh_ref, w_ref, o_ref, h_ref):
    @pl.when(pl.program_id(1) == 0)
    def _():
        x = x_ref[...]
        y = x * lax.rsqrt(jnp.mean(x * x, axis=-1, keepdims=True) + EPS)
        h = (y * g_ref[...]) * (1.0 + sc_ref[0]) + sh_ref[0]
        h_ref[...] = h.astype(BF16)

    o_ref[...] = jnp.dot(h_ref[...], w_ref[...], preferred_element_type=F32).astype(o_ref.dtype)


def _norm_proj(x2, g, scale, shift, w, S, out_dtype, tm, tn, name):
    M, D = x2.shape
    N = w.shape[1]
    per_b = S // tm
    return pl.pallas_call(
        _norm_proj_kernel,
        grid=(M // tm, N // tn),
        in_specs=[pl.BlockSpec((tm, D), lambda m, n: (m, 0)),
                  pl.BlockSpec((1, D), lambda m, n: (0, 0)),
                  pl.BlockSpec((1, 1, D), lambda m, n: (m // per_b, 0, 0)),
                  pl.BlockSpec((1, 1, D), lambda m, n: (m // per_b, 0, 0)),
                  pl.BlockSpec((D, tn), lambda m, n: (0, n))],
        out_specs=pl.BlockSpec((tm, tn), lambda m, n: (m, n)),
        out_shape=jax.ShapeDtypeStruct((M, N), out_dtype),
        scratch_shapes=[pltpu.VMEM((tm, D), BF16)],
        compiler_params=_params("arbitrary", "arbitrary"),
        name=name,
    )(x2, g, scale, shift, w)


def _prep_kernel(misc_ref, ckv_ref, ik_ref, ikr_ref, cos_ref, sin_ref, bf_ref, gkv_ref, wkv_ref,
                 ft_ref, kk_ref, vvt_ref, ikk_ref, carry_ref):
    @pl.when(pl.program_id(1) == 0)
    def _():
        carry_ref[...] = jnp.zeros_like(carry_ref)

    tp = misc_ref.shape[1]
    xf = misc_ref[0] + bf_ref[...]
    logf = jnp.minimum(xf, 0.0) - jnp.log1p(jnp.exp(-jnp.abs(xf)))
    r = lax.broadcasted_iota(I32, (tp, tp), 0)
    c = lax.broadcasted_iota(I32, (tp, tp), 1)
    tri = jnp.where(r >= c, 1.0, 0.0).astype(F32)
    cum = jnp.dot(tri, logf, preferred_element_type=F32, precision=HIGHEST) + carry_ref[...]
    carry_ref[...] = cum[tp - 1:tp, :]
    ft_ref[0] = cum.T[:N_HEADS, :]

    cos = cos_ref[0]
    sin = sin_ref[0]
    ckv = ckv_ref[0]
    nrm = ckv * lax.rsqrt(jnp.mean(ckv * ckv, axis=-1, keepdims=True) + EPS) * gkv_ref[...]
    kv = jnp.dot(nrm.astype(BF16), wkv_ref[...], preferred_element_type=F32)
    kk_ref[0] = (kv[:, :LANES] * cos + kv[:, LANES:2 * LANES] * sin).astype(BF16)
    v_t = kv[:, 2 * LANES:].T.astype(BF16)
    chunk = vvt_ref.shape[3]
    for i in range(vvt_ref.shape[1]):
        vvt_ref[0, i] = v_t[:, i * chunk:(i + 1) * chunk]
    ikk_ref[0] = (ik_ref[0] * cos + ikr_ref[0] * sin).astype(BF16)


def _prep(zs, cos, sin, bf, gkv, wkv, tp, tkv):
    B, S, _ = zs.shape
    assert tp % tkv == 0
    blk = lambda col: pl.BlockSpec((1, tp, LANES), lambda b, j: (b, j, col))
    row = pl.BlockSpec((1, tp, LANES), lambda b, j: (b, j, 0))
    return pl.pallas_call(
        _prep_kernel,
        grid=(B, S // tp),
        in_specs=[blk(_MISC), blk(_CKV), blk(_IK), blk(_IKR), row, row,
                  pl.BlockSpec((1, LANES), lambda b, j: (0, 0)),
                  pl.BlockSpec((1, LANES), lambda b, j: (0, 0)),
                  pl.BlockSpec((KV_RANK, 3 * LANES), lambda b, j: (0, 0))],
        out_specs=[pl.BlockSpec((1, N_HEADS, tp), lambda b, j: (b, 0, j)),
                   row, pl.BlockSpec((1, tp // tkv, LANES, tkv), lambda b, j: (b, j, 0, 0)), row],
        out_shape=[jax.ShapeDtypeStruct((B, N_HEADS, S), F32),
                   jax.ShapeDtypeStruct((B, S, LANES), BF16),
                   jax.ShapeDtypeStruct((B, S // tkv, LANES, tkv), BF16),
                   jax.ShapeDtypeStruct((B, S, LANES), BF16)],
        scratch_shapes=[pltpu.VMEM((1, LANES), F32)],
        compiler_params=_params("arbitrary", "arbitrary"),
        name="prep",
    )(zs, zs, zs, zs, cos, sin, bf, gkv, wkv)


def _head_masks():
    lane = lax.broadcasted_iota(I32, (1, LANES), 1)
    lo = lane < HEAD_DIM
    return lo, jnp.logical_not(lo)


def _silu(g):
    return g * jax.nn.sigmoid(g)


def _pair_heads(q, masks):
    slabs = []
    for p in range(q.shape[1] // LANES):
        blk = q[:, p * LANES:(p + 1) * LANES]
        slabs += [jnp.where(masks[hh], blk, jnp.zeros_like(blk)) for hh in range(2)]
    return slabs


def _join_heads(outs, masks):
    return jnp.concatenate([jnp.where(masks[0], outs[2 * p], outs[2 * p + 1]) for p in range(len(outs) // 2)], axis=1)


def _fox_kernel(q_ref, k_ref, v_ref, g_ref, ft_ref, o_ref, *, tq, tk):
    qi = pl.program_id(2)
    masks = _head_masks()
    assert tq % tk == 0 or tk % tq == 0
    n_full = (qi * tq) // tk
    rows = qi * tq + lax.broadcasted_iota(I32, (tq, tk), 0)
    col0 = lax.broadcasted_iota(I32, (tq, tk), 1)
    qm = _pair_heads(q_ref[0], masks)
    nh = len(qm)
    f0 = jnp.max(ft_ref[0, 0, n_full], axis=1, keepdims=True)

    def step(j, carry, masked):
        start = pl.multiple_of(j * tk, tk)
        k = k_ref[0, pl.ds(start, tk), :]
        v = v_ref[0, pl.ds(start, tk), :]
        brow = f0 - ft_ref[0, 0, j]
        out = []
        for h in range(nh):
            m, l, acc = carry[h]
            lanes = slice((h // 2) * LANES, (h // 2 + 1) * LANES)
            s = lax.dot_general(qm[h], k[:, lanes], _NT, preferred_element_type=F32) + brow[h:h + 1, :]
            if masked:
                s = jnp.where(rows >= j * tk + col0, s, NEG)
            m_new = jnp.maximum(m, jnp.max(s, axis=1, keepdims=True))
            alpha = jnp.exp(m - m_new)
            p = jnp.exp(s - m_new)
            l = alpha * l + jnp.sum(p, axis=1, keepdims=True)
            acc = alpha * acc + jnp.dot(p.astype(BF16), v[:, lanes], preferred_element_type=F32)
            out.append((m_new, l, acc))
        return tuple(out)

    init = (jnp.full((tq, 1), NEG, F32), jnp.zeros((tq, 1), F32), jnp.zeros((tq, LANES), F32))
    carry = lax.fori_loop(0, n_full, functools.partial(step, masked=False), (init,) * nh)
    for d in range(max(1, tq // tk)):
        carry = step(n_full + d, carry, True)
    y = _join_heads([acc / l for _, l, acc in carry], masks)
    o_ref[0] = (y * _silu(g_ref[0].astype(F32))).astype(o_ref.dtype)


def _fox(zm, ft, tq, tk, npair):
    B, S, _ = zm.shape
    w = npair * LANES
    ng = 4 // npair
    col = lambda base: (lambda b, h, i: (b, i, base * ng + h))
    seq = lambda base: (lambda b, h, i: (b, 0, base * ng + h))
    ft5 = ft.reshape(B, ng, 2 * npair, S // tk, tk).transpose(0, 1, 3, 2, 4)
    return pl.pallas_call(
        functools.partial(_fox_kernel, tq=tq, tk=tk),
        grid=(B, ng, S // tq),
        in_specs=[pl.BlockSpec((1, tq, w), col(_FQ)),
                  pl.BlockSpec((1, S, w), seq(_FK)),
                  pl.BlockSpec((1, S, w), seq(_FV)),
                  pl.BlockSpec((1, tq, w), col(_FG)),
                  pl.BlockSpec((1, 1, S // tk, 2 * npair, tk), lambda b, h, i: (b, h, 0, 0, 0))],
        out_specs=pl.BlockSpec((1, tq, w), lambda b, h, i: (b, i, h)),
        out_shape=jax.ShapeDtypeStruct((B, S, WIDTH), BF16),
        compiler_params=_params("arbitrary", "arbitrary", "arbitrary"),
        name="fox_attention",
    )(zm, zm, zm, zm, ft5)


def _sb_kernel(q_ref, k_ref, v_ref, g_ref, sfx_ref, o_ref, *, tq, tk):
    qi = pl.program_id(2)
    masks = _head_masks()
    qm = _pair_heads(q_ref[0], masks)
    nh = len(qm)
    n_off = qi * (tq // tk)
    rows = lax.broadcasted_iota(I32, (tq, tk), 0)
    col0 = lax.broadcasted_iota(I32, (tq, tk), 1)

    def step(c, carry, masked):
        start = pl.multiple_of(c * tk, tk)
        k = k_ref[0, pl.ds(start, tk), :]
        v = v_ref[0, pl.ds(start, tk), :]
        if masked:
            valid = rows > (c - n_off) * tk + col0
        out = []
        for h in range(nh):
            later, acc = carry[h]
            lanes = slice((h // 2) * LANES, (h // 2 + 1) * LANES)
            z = lax.dot_general(qm[h], k[:, lanes], _NT, preferred_element_type=F32)
            log1m = -(jnp.maximum(z, 0.0) + jnp.log(1.0 + jnp.exp(-jnp.abs(z))))
            if masked:
                log1m = jnp.where(valid, log1m, 0.0)
            hi = log1m.astype(BF16)
            lo = (log1m - hi.astype(F32)).astype(BF16)
            incl = jnp.dot(jnp.concatenate([hi, lo], axis=1), sfx_ref[...], preferred_element_type=F32)
            a = jnp.exp(z + incl + later)
            if masked:
                a = jnp.where(valid, a, 0.0)
            acc = acc + jnp.dot(a.astype(BF16), v[:, lanes], preferred_element_type=F32)
            out.append((later + incl[:, :1], acc))
        return tuple(out)

    carry = ((jnp.zeros((tq, 1), F32), jnp.zeros((tq, LANES), F32)),) * nh
    for d in reversed(range(tq // tk)):
        carry = step(n_off + d, carry, True)
    carry = lax.fori_loop(0, n_off, lambda i, cr: step(n_off - 1 - i, cr, False), carry)
    y = _join_heads([acc for _, acc in carry], masks)
    o_ref[0] = (y * _silu(g_ref[0].astype(F32))).astype(o_ref.dtype)


def _sb(zm, tq, tk, npair):
    B, S, _ = zm.shape
    assert tq % tk == 0
    w = npair * LANES
    ng = 4 // npair
    col = lambda base: (lambda b, h, i: (b, i, base * ng + h))
    seq = lambda base: (lambda b, h, i: (b, 0, base * ng + h))
    j = jnp.arange(2 * tk, dtype=I32)[:, None] % tk
    sfx = (j >= jnp.arange(tk, dtype=I32)[None, :]).astype(BF16)
    return pl.pallas_call(
        functools.partial(_sb_kernel, tq=tq, tk=tk),
        grid=(B, ng, S // tq),
        in_specs=[pl.BlockSpec((1, tq, w), col(_SQ)),
                  pl.BlockSpec((1, S, w), seq(_SK)),
                  pl.BlockSpec((1, S, w), seq(_SV)),
                  pl.BlockSpec((1, tq, w), col(_SG)),
                  pl.BlockSpec((2 * tk, tk), lambda b, h, i: (0, 0))],
        out_specs=pl.BlockSpec((1, tq, w), lambda b, h, i: (b, i, h)),
        out_shape=jax.ShapeDtypeStruct((B, S, WIDTH), BF16),
        compiler_params=_params("arbitrary", "arbitrary", "arbitrary"),
        name="stick_breaking_attention",
    )(zm, zm, zm, zm, sfx)


def _dsa_kernel(iq_ref, iqr_ref, misc_ref, cos_ref, sin_ref, dq_ref, dqr_ref, dg_ref,
                ikk_ref, kk_ref, vvt_ref, o_ref, key_ref, bias_ref, cut_ref, *, tq, tk, ktop, seq_bits):
    qi = pl.program_id(1)
    q0 = qi * tq
    nch = (q0 + tq + tk - 1) // tk
    masks = _head_masks()
    cos = cos_ref[0]
    sin = sin_ref[0]
    cos4 = jnp.concatenate([cos] * 4, axis=1)
    sin4 = jnp.concatenate([sin] * 4, axis=1)
    iq = iq_ref[0] * cos4 + iqr_ref[0] * sin4
    qd = dq_ref[0].astype(F32) * cos4 + dqr_ref[0].astype(F32) * sin4

    def slab_t(x, h):
        blk = x[:, (h // 2) * LANES:(h // 2 + 1) * LANES]
        return jnp.where(masks[h % 2], blk, 0.0).T.astype(BF16)

    iq_t = [jnp.concatenate([slab_t(iq, 2 * p), slab_t(iq, 2 * p + 1)], axis=1) for p in range(N_HEADS // 2)]
    qs_t = jnp.concatenate([slab_t(qd, h) for h in range(N_HEADS)], axis=1)
    w_t = misc_ref[0].T[N_HEADS:2 * N_HEADS, :] * (N_HEADS ** -0.5)
    w_pair = [jnp.concatenate([w_t[2 * p:2 * p + 1], w_t[2 * p + 1:2 * p + 2]], axis=1) for p in range(N_HEADS // 2)]

    key_idx0 = lax.broadcasted_iota(I32, (tk, tq), 0)
    query_idx = q0 + lax.broadcasted_iota(I32, (tk, tq), 1)

    def score_chunk(c, carry):
        ik = ikk_ref[0, pl.ds(pl.multiple_of(c * tk, tk), tk), :]
        acc2 = jnp.zeros((tk, 2 * tq), F32)
        for p in range(N_HEADS // 2):
            s = jnp.dot(ik, iq_t[p], preferred_element_type=F32)
            acc2 = acc2 + jnp.maximum(s, 0.0) * w_pair[p]
        acc = acc2[:, :tq] + acc2[:, tq:]
        acc = jnp.where(acc == 0.0, 0.0, acc)
        bits = lax.bitcast_convert_type(acc, I32)
        key = bits ^ (lax.shift_right_arithmetic(bits, 31) & 0x7FFFFFFF)
        key = jnp.maximum(key, INT_MIN + 1)
        key_ref[c] = jnp.where(c * tk + key_idx0 <= query_idx, key, INT_MIN)
        return carry

    lax.fori_loop(0, nch, score_chunk, 0)

    n_acc = 4

    def count(pred):
        def body(c, cnt):
            hit = jnp.where(pred(key_ref[c], c * tk + key_idx0), 1.0, 0.0)
            return cnt + hit.reshape(n_acc, tk // (8 * n_acc), 8, tq).sum(axis=1)
        cnt = lax.fori_loop(0, nch, body, jnp.zeros((n_acc, 8, tq), F32))
        return jnp.sum(cnt.reshape(n_acc * 8, tq), axis=0, keepdims=True)

    def bit_body(b, prefix):
        cand_u = prefix | lax.shift_left(jnp.int32(1), 31 - b)
        cand = jnp.broadcast_to(cand_u ^ INT_MIN, (tk, tq))
        n = count(lambda key, idx: key >= cand)
        return jnp.where(n >= ktop, cand_u, prefix)

    prefix = lax.fori_loop(0, 32, bit_body, jnp.zeros((1, tq), I32))
    thr = jnp.broadcast_to(prefix ^ INT_MIN, (tk, tq))
    n_gt = count(lambda key, idx: key > thr)
    n_ge = count(lambda key, idx: key >= thr)
    need = ktop - n_gt

    cut_ref[...] = jnp.full(cut_ref.shape, 2 ** seq_bits, I32)

    @pl.when(jnp.max(n_ge) > ktop)
    def _():
        def idx_body(b, lo):
            cand_i = lo | lax.shift_left(jnp.int32(1), seq_bits - 1 - b)
            cand = jnp.broadcast_to(cand_i, (tk, tq))
            g = count(lambda key, idx: (key == thr) & (idx < cand))
            return jnp.where(g < need, cand_i, lo)
        lo = lax.fori_loop(0, seq_bits, idx_body, jnp.zeros((1, tq), I32))
        cut_ref[...] = jnp.broadcast_to(lo, cut_ref.shape)

    cut = jnp.broadcast_to(cut_ref[:1, :], (tk, tq))

    def bias_chunk(c, carry):
        key = key_ref[c]
        sel = (key > thr) | ((key == thr) & (c * tk + key_idx0 <= cut))
        sel = sel & (key > INT_MIN)
        bias_ref[c] = jnp.where(sel, 0.0, NEG)
        return carry

    lax.fori_loop(0, nch, bias_chunk, 0)

    def att(c, carry):
        m, l, acc = carry
        start = pl.multiple_of(c * tk, tk)
        k = kk_ref[0, pl.ds(start, tk), :]
        v_t = vvt_ref[0, c]
        s = jnp.dot(k, qs_t, preferred_element_type=F32)
        s = s + jnp.concatenate([bias_ref[c]] * N_HEADS, axis=1)
        m_new = jnp.maximum(m, jnp.max(s, axis=0, keepdims=True))
        alpha = jnp.exp(m - m_new)
        pr = jnp.exp(s - m_new)
        l = alpha * l + jnp.sum(pr, axis=0, keepdims=True)
        acc = alpha * acc + jnp.dot(v_t, pr.astype(BF16), preferred_element_type=F32)
        return m_new, l, acc

    wide = N_HEADS * tq
    init = (jnp.full((1, wide), NEG, F32), jnp.zeros((1, wide), F32), jnp.zeros((LANES, wide), F32))
    _, l, acc = lax.fori_loop(0, nch, att, init)
    out_t = acc / l
    own = lax.broadcasted_iota(I32, (LANES, tq), 0) < HEAD_DIM
    for p in range(N_HEADS // 2):
        y_t = jnp.where(own, out_t[:, 2 * p * tq:(2 * p + 1) * tq], out_t[:, (2 * p + 1) * tq:(2 * p + 2) * tq])
        g = dg_ref[0, :, p * LANES:(p + 1) * LANES].astype(F32)
        o_ref[0, :, p * LANES:(p + 1) * LANES] = (y_t.T * _silu(g)).astype(o_ref.dtype)


def _dsa(zs, zm, cos, sin, ikk, kk, vvt, tq, ktop):
    B, S, _ = zs.shape
    tk = vvt.shape[3]
    assert tq == LANES
    wide = lambda col: pl.BlockSpec((1, tq, WIDTH), lambda b, i: (b, i, col))
    row = pl.BlockSpec((1, tq, LANES), lambda b, i: (b, i, 0))
    seq = pl.BlockSpec((1, S, LANES), lambda b, i: (b, 0, 0))
    seq_bits = max(1, (S - 1).bit_length())
    return pl.pallas_call(
        functools.partial(_dsa_kernel, tq=tq, tk=tk, ktop=ktop, seq_bits=seq_bits),
        grid=(B, S // tq),
        in_specs=[wide(_IQ), wide(_IQR),
                  pl.BlockSpec((1, tq, LANES), lambda b, i: (b, i, _MISC)),
                  row, row, wide(_DQ), wide(_DQR), wide(_DG), seq, seq,
                  pl.BlockSpec((1, S // tk, LANES, tk), lambda b, i: (b, 0, 0, 0))],
        out_specs=pl.BlockSpec((1, tq, WIDTH), lambda b, i: (b, i, 0)),
        out_shape=jax.ShapeDtypeStruct((B, S, WIDTH), BF16),
        scratch_shapes=[pltpu.VMEM((S // tk, tk, tq), I32),
                        pltpu.VMEM((S // tk, tk, tq), F32),
                        pltpu.VMEM((8, tq), I32)],
        compiler_params=_params("arbitrary", "arbitrary"),
        name="dsa_attention",
    )(zs, zs, zs, cos, sin, zm, zm, zm, ikk, kk, vvt)


def _merge_kernel(x_ref, gate_ref, yf_ref, yd_ref, ys_ref, m0_ref, m1_ref, m2_ref,
                  wf_ref, wd_ref, ws_ref, wo_ref, gfin_ref, o_ref, *, final):
    def branch(y_ref, w_ref, m_ref):
        proj = jnp.dot(y_ref[...], w_ref[...], preferred_element_type=F32)
        return jax.nn.sigmoid(m_ref[...].astype(F32)) * proj

    mixed = branch(yf_ref, wf_ref, m0_ref) + branch(yd_ref, wd_ref, m1_ref) + branch(ys_ref, ws_ref, m2_ref)
    o = x_ref[...] + gate_ref[0] * jnp.dot(mixed.astype(BF16), wo_ref[...], preferred_element_type=F32)
    if final:
        o = o * lax.rsqrt(jnp.mean(o * o, axis=-1, keepdims=True) + EPS) * gfin_ref[...]
    o_ref[...] = o


def _merge(x2, gate, yf, yd, ys, zm2, wf, wd, ws, wo, gfin, S, tm, final):
    M, D = x2.shape
    per_b = S // tm
    ysp = pl.BlockSpec((tm, WIDTH), lambda m: (m, 0))
    msp = lambda j: pl.BlockSpec((tm, D), lambda m: (m, j))
    wsp = pl.BlockSpec((WIDTH, D), lambda m: (0, 0))
    return pl.pallas_call(
        functools.partial(_merge_kernel, final=final),
        grid=(M // tm,),
        in_specs=[pl.BlockSpec((tm, D), lambda m: (m, 0)),
                  pl.BlockSpec((1, 1, D), lambda m: (m // per_b, 0, 0)),
                  ysp, ysp, ysp, msp(0), msp(1), msp(2), wsp, wsp, wsp,
                  pl.BlockSpec((D, D), lambda m: (0, 0)),
                  pl.BlockSpec((1, D), lambda m: (0, 0))],
        out_specs=pl.BlockSpec((tm, D), lambda m: (m, 0)),
        out_shape=jax.ShapeDtypeStruct((M, D), F32),
        compiler_params=_params("arbitrary"),
        name="merge_out",
    )(x2, gate, yf, yd, ys, zm2, zm2, zm2, wf, wd, ws, wo, gfin)


def _rot_cols(w):
    d, n = w.shape
    w4 = w.reshape(d, n // HEAD_DIM, 2, HEAD_DIM // 2)
    return jnp.concatenate([-w4[:, :, 1:], w4[:, :, :1]], axis=2).reshape(d, n)


def _split_w_in(w):
    sizes = (WIDTH, WIDTH, WIDTH, N_HEADS, WIDTH,
             WIDTH, KV_RANK, WIDTH, HEAD_DIM, N_HEADS, WIDTH,
             WIDTH, WIDTH, WIDTH, WIDTH, w.shape[1] - (11 * WIDTH + 2 * N_HEADS + KV_RANK + HEAD_DIM))
    parts, off = [], 0
    for s in sizes:
        parts.append(w[:, off:off + s])
        off += s
    return parts


def _layer_weights(w_in, w_kv_up):
    (fq, fk, fv, ff, fg, dq, dckv, diq, dik, diw, dg, sq, sk, sv, sg, merge) = _split_w_in(w_in)
    scale = HEAD_DIM ** -0.5
    fq, dq, sq, diq = fq * scale, dq * scale, sq * scale, diq * scale
    w_main = jnp.concatenate([merge, fq, fk, fv, fg, dq, _rot_cols(dq), dg, sq, sk, sv, sg], axis=1)
    pad = jnp.zeros((w_in.shape[0], LANES - 2 * N_HEADS), w_in.dtype)
    dikr = _rot_cols(dik)
    w_small = jnp.concatenate([diq, _rot_cols(diq), ff, diw, pad, dckv, dik, dik, dikr, dikr], axis=1)
    wk, wv = w_kv_up[:, :HEAD_DIM], w_kv_up[:, HEAD_DIM:]
    wkr = _rot_cols(wk)
    w_kv = jnp.concatenate([wk, wk, wkr, wkr, wv, wv], axis=1)
    return w_main.astype(BF16), w_small.astype(BF16), w_kv.astype(BF16)


def kernel(x, c, positions, w_ada, b_ada, g_norm, w_in, b_fgt, g_kv, w_kv_up, w_br_fox, w_br_dsa, w_br_sb, w_out, g_final):
    B, S, D = x.shape
    depth = w_in.shape[0]
    assert D == 2 * WIDTH and S % 256 == 0
    ktop = min(TOPK_MAX, S // 4)
    tm = min(1024, S)
    tp = min(512, S)
    tq_att = 256
    tk_fox = min(512, S)

    half = HEAD_DIM // 2
    inv_freq = ROPE_THETA ** (-jnp.arange(half, dtype=F32) / half)
    ang = positions.astype(F32)[..., None] * inv_freq
    cos = jnp.tile(jnp.cos(ang), (1, 1, LANES // half))
    sin = jnp.tile(jnp.sin(ang), (1, 1, LANES // half))

    mod = _modulation(c, w_ada, b_ada)
    x2 = x.reshape(B * S, D)
    for l in range(depth):
        shift = mod[l, :, None, :D]
        scale = mod[l, :, None, D:2 * D]
        gate = mod[l, :, None, 2 * D:]
        w_main, w_small, w_kv = _layer_weights(w_in[l], w_kv_up[l])
        g = g_norm[l].reshape(1, D)
        zm2 = _norm_proj(x2, g, scale, shift, w_main, S, BF16, tm, 512, "norm_proj_main")
        zs2 = _norm_proj(x2, g, scale, shift, w_small, S, F32, tm, 512, "norm_proj_small")
        zm = zm2.reshape(B, S, N_MAIN)
        zs = zs2.reshape(B, S, N_SMALL)
        bf = jnp.zeros((1, LANES), F32).at[0, :N_HEADS].set(b_fgt[l])
        ft, kk, vvt, ikk = _prep(zs, cos, sin, bf, g_kv[l].reshape(1, KV_RANK), w_kv, tp, min(512, S))
        y_fox = _fox(zm, ft, min(512, S), tk_fox, 1)
        y_sb = _sb(zm, min(512, S), 256, 1)
        y_dsa = _dsa(zs, zm, cos, sin, ikk, kk, vvt, LANES, ktop)
        x2 = _merge(x2, gate, y_fox.reshape(B * S, WIDTH), y_dsa.reshape(B * S, WIDTH),
                    y_sb.reshape(B * S, WIDTH), zm2,
                    w_br_fox[l].astype(BF16), w_br_dsa[l].astype(BF16), w_br_sb[l].astype(BF16),
                    w_out[l].astype(BF16), g_final.reshape(1, D), S, min(512, S), l == depth - 1)
    return x2.reshape(B, S, D)
```

```python
import functools

import jax
import jax.numpy as jnp
from jax import lax
from jax.experimental import pallas as pl
from jax.experimental.pallas import tpu as pltpu

F32 = jnp.float32
BF16 = jnp.bfloat16
I32 = jnp.int32
HIGHEST = lax.Precision.HIGHEST

HEAD_DIM = 64
N_HEADS = 8
WIDTH = N_HEADS * HEAD_DIM
KV_RANK = 128
TOPK_MAX = 256
ROPE_THETA = 10000.0
EPS = 1e-6
LANES = 128
NEG = -1e30
INT_MIN = -(2 ** 31)
VMEM_LIMIT = 56 * 1024 * 1024

_MERGE, _FQ, _FK, _FV, _FG, _DQ, _DQR, _DG, _SQ, _SK, _SV, _SG = 0, 6, 7, 8, 9, 10, 11, 12, 13, 14, 15, 16
N_MAIN = 17 * WIDTH
_IQ, _IQR = 0, 1
_MISC, _CKV, _IK, _IKR = 8, 9, 10, 11
N_SMALL = 12 * LANES


def _params(*sem):
    return pltpu.CompilerParams(dimension_semantics=sem, vmem_limit_bytes=VMEM_LIMIT)


def _mod_kernel(c_ref, w_ref, b_ref, o_ref):
    c = c_ref[...]
    sc = c * jax.nn.sigmoid(c)
    o_ref[0] = jnp.dot(sc, w_ref[0], preferred_element_type=F32, precision=HIGHEST) + b_ref[0]


def _modulation(c, w_ada, b_ada):
    L, D, N = w_ada.shape
    B = c.shape[0]
    tn = 512
    return pl.pallas_call(
        _mod_kernel,
        grid=(L, N // tn),
        in_specs=[pl.BlockSpec((B, D), lambda l, n: (0, 0)),
                  pl.BlockSpec((1, D, tn), lambda l, n: (l, 0, n)),
                  pl.BlockSpec((1, 1, tn), lambda l, n: (l, 0, n))],
        out_specs=pl.BlockSpec((1, B, tn), lambda l, n: (l, 0, n)),
        out_shape=jax.ShapeDtypeStruct((L, B, N), F32),
        compiler_params=_params("arbitrary", "arbitrary"),
        name="adaln_mod",
    )(c, w_ada, b_ada.reshape(L, 1, N))


def _norm_proj_kernel(x_ref, g_ref, sc_ref, sh_ref, w_ref, o_ref, h_ref):
    @pl.when(pl.program_id(1) == 0)
    def _():
        x = x_ref[...]
        y = x * lax.rsqrt(jnp.mean(x * x, axis=-1, keepdims=True) + EPS)
        h = (y * g_ref[...]) * (1.0 + sc_ref[0]) + sh_ref[0]
        h_ref[...] = h.astype(BF16)

    o_ref[...] = jnp.dot(h_ref[...], w_ref[...], preferred_element_type=F32).astype(o_ref.dtype)


def _norm_proj(x2, g, scale, shift, w, S, out_dtype, tm, tn, name):
    M, D = x2.shape
    N = w.shape[1]
    per_b = S // tm
    return pl.pallas_call(
        _norm_proj_kernel,
        grid=(M // tm, N // tn),
        in_specs=[pl.BlockSpec((tm, D), lambda m, n: (m, 0)),
                  pl.BlockSpec((1, D), lambda m, n: (0, 0)),
                  pl.BlockSpec((1, 1, D), lambda m, n: (m // per_b, 0, 0)),
                  pl.BlockSpec((1, 1, D), lambda m, n: (m // per_b, 0, 0)),
                  pl.BlockSpec((D, tn), lambda m, n: (0, n))],
        out_specs=pl.BlockSpec((tm, tn), lambda m, n: (m, n)),
        out_shape=jax.ShapeDtypeStruct((M, N), out_dtype),
        scratch_shapes=[pltpu.VMEM((tm, D), BF16)],
        compiler_params=_params("arbitrary", "arbitrary"),
        name=name,
    )(x2, g, scale, shift, w)


def _store_transposed(x, out_ref, lead=()):
    x_t = x.astype(F32).T.astype(out_ref.dtype)
    chunk = out_ref.shape[-1]
    for i in range(x.shape[0] // chunk):
        out_ref[lead + (i,)] = x_t[:, i * chunk:(i + 1) * chunk]


def _prep_kernel(misc_ref, ckv_ref, ik_ref, ikr_ref, cos_ref, sin_ref, bf_ref, gkv_ref, wkv_ref, fv_ref, sv_ref,
                 fcol_ref, kk_ref, vvt_ref, ikk_ref, fvt_ref, svt_ref, carry_ref):
    @pl.when(pl.program_id(1) == 0)
    def _():
        carry_ref[...] = jnp.zeros_like(carry_ref)

    for p in range(N_HEADS // 2):
        _store_transposed(fv_ref[0, :, p * LANES:(p + 1) * LANES], fvt_ref, (0, p))
        _store_transposed(sv_ref[0, :, p * LANES:(p + 1) * LANES], svt_ref, (0, p))

    tp = misc_ref.shape[1]
    xf = misc_ref[0] + bf_ref[...]
    logf = jnp.minimum(xf, 0.0) - jnp.log1p(jnp.exp(-jnp.abs(xf)))
    r = lax.broadcasted_iota(I32, (tp, tp), 0)
    c = lax.broadcasted_iota(I32, (tp, tp), 1)
    tri = jnp.where(r >= c, 1.0, 0.0).astype(F32)
    cum = jnp.dot(tri, logf, preferred_element_type=F32, precision=HIGHEST) + carry_ref[...]
    carry_ref[...] = cum[tp - 1:tp, :]
    fcol_ref[0] = cum[:, :N_HEADS]

    cos = cos_ref[0]
    sin = sin_ref[0]
    ckv = ckv_ref[0]
    nrm = ckv * lax.rsqrt(jnp.mean(ckv * ckv, axis=-1, keepdims=True) + EPS) * gkv_ref[...]
    kv = jnp.dot(nrm.astype(BF16), wkv_ref[...], preferred_element_type=F32)
    kk_ref[0] = (kv[:, :LANES] * cos + kv[:, LANES:2 * LANES] * sin).astype(BF16)
    _store_transposed(kv[:, 2 * LANES:], vvt_ref, (0,))
    ikk_ref[0] = (ik_ref[0] * cos + ikr_ref[0] * sin).astype(BF16)


def _prep(zs, zm, cos, sin, bf, gkv, wkv, tp, tk_dsa, tk_fox, tk_sb):
    B, S, _ = zs.shape
    assert tp % tk_dsa == 0 and tp % tk_fox == 0 and tp % tk_sb == 0
    blk = lambda col: pl.BlockSpec((1, tp, LANES), lambda b, j: (b, j, col))
    row = pl.BlockSpec((1, tp, LANES), lambda b, j: (b, j, 0))
    wide = lambda col: pl.BlockSpec((1, tp, WIDTH), lambda b, j: (b, j, col))
    vt_spec = lambda tk: pl.BlockSpec((1, N_HEADS // 2, tp // tk, LANES, tk), lambda b, j: (b, 0, j, 0, 0))
    vt_shape = lambda tk: jax.ShapeDtypeStruct((B, N_HEADS // 2, S // tk, LANES, tk), BF16)
    return pl.pallas_call(
        _prep_kernel,
        grid=(B, S // tp),
        in_specs=[blk(_MISC), blk(_CKV), blk(_IK), blk(_IKR), row, row,
                  pl.BlockSpec((1, LANES), lambda b, j: (0, 0)),
                  pl.BlockSpec((1, LANES), lambda b, j: (0, 0)),
                  pl.BlockSpec((KV_RANK, 3 * LANES), lambda b, j: (0, 0)),
                  wide(_FV), wide(_SV)],
        out_specs=[pl.BlockSpec((1, tp, N_HEADS), lambda b, j: (b, j, 0)),
                   row, pl.BlockSpec((1, tp // tk_dsa, LANES, tk_dsa), lambda b, j: (b, j, 0, 0)), row,
                   vt_spec(tk_fox), vt_spec(tk_sb)],
        out_shape=[jax.ShapeDtypeStruct((B, S, N_HEADS), F32),
                   jax.ShapeDtypeStruct((B, S, LANES), BF16),
                   jax.ShapeDtypeStruct((B, S // tk_dsa, LANES, tk_dsa), BF16),
                   jax.ShapeDtypeStruct((B, S, LANES), BF16),
                   vt_shape(tk_fox), vt_shape(tk_sb)],
        scratch_shapes=[pltpu.VMEM((1, LANES), F32)],
        compiler_params=_params("arbitrary", "arbitrary"),
        name="prep",
    )(zs, zs, zs, zs, cos, sin, bf, gkv, wkv, zm, zm)


def _head_masks():
    lane = lax.broadcasted_iota(I32, (1, LANES), 1)
    lo = lane < HEAD_DIM
    return lo, jnp.logical_not(lo)


def _silu(g):
    return g * jax.nn.sigmoid(g)


def _pair_queries_t(q, masks):
    q = q.astype(F32)
    return jnp.concatenate([jnp.where(masks[hh], q, 0.0).T.astype(BF16) for hh in range(2)], axis=1)


def _pair_output(out_t, g, tq):
    own = lax.broadcasted_iota(I32, (LANES, tq), 0) < HEAD_DIM
    return jnp.where(own, out_t[:, :tq], out_t[:, tq:]).T * _silu(g.astype(F32))


def _fox_kernel(q_ref, k_ref, vt_ref, g_ref, fcol_ref, o_ref, *, tq, tk):
    qi = pl.program_id(2)
    assert tq % tk == 0 or tk % tq == 0
    n_full = (qi * tq) // tk
    qs_t = _pair_queries_t(q_ref[0], _head_masks())
    key_idx0 = lax.broadcasted_iota(I32, (tk, tq), 0)
    query_idx = qi * tq + lax.broadcasted_iota(I32, (tk, tq), 1)
    f0 = fcol_ref[0, 0, pl.ds(pl.multiple_of(n_full * tk, tk), 1), :]

    def step(j, carry, masked):
        m, l, acc = carry
        start = pl.multiple_of(j * tk, tk)
        k = k_ref[0, pl.ds(start, tk), :]
        b = f0 - fcol_ref[0, 0, pl.ds(start, tk), :]
        bias = jnp.concatenate([jnp.broadcast_to(b[:, hh:hh + 1], (tk, tq)) for hh in range(2)], axis=1)
        s = jnp.dot(k, qs_t, preferred_element_type=F32) + bias
        if masked:
            valid = j * tk + key_idx0 <= query_idx
            s = jnp.where(jnp.concatenate([valid, valid], axis=1), s, NEG)
        m_new = jnp.maximum(m, jnp.max(s, axis=0, keepdims=True))
        alpha = jnp.exp(m - m_new)
        p = jnp.exp(s - m_new)
        l = alpha * l + jnp.sum(p, axis=0, keepdims=True)
        acc = alpha * acc + jnp.dot(vt_ref[0, 0, j], p.astype(BF16), preferred_element_type=F32)
        return m_new, l, acc

    init = (jnp.full((1, 2 * tq), NEG, F32), jnp.zeros((1, 2 * tq), F32), jnp.zeros((LANES, 2 * tq), F32))
    carry = lax.fori_loop(0, n_full, functools.partial(step, masked=False), init)
    for d in range(max(1, tq // tk)):
        carry = step(n_full + d, carry, True)
    _, l, acc = carry
    o_ref[0] = _pair_output(acc / l, g_ref[0], tq).astype(o_ref.dtype)


def _fox(zm, fcol, fvt, tq):
    B, S, _ = zm.shape
    tk = fvt.shape[4]
    col = lambda base: (lambda b, h, i: (b, i, base * 4 + h))
    f4 = fcol.reshape(B, S, 4, 2).transpose(0, 2, 1, 3)
    return pl.pallas_call(
        functools.partial(_fox_kernel, tq=tq, tk=tk),
        grid=(B, 4, S // tq),
        in_specs=[pl.BlockSpec((1, tq, LANES), col(_FQ)),
                  pl.BlockSpec((1, S, LANES), lambda b, h, i: (b, 0, _FK * 4 + h)),
                  pl.BlockSpec((1, 1, S // tk, LANES, tk), lambda b, h, i: (b, h, 0, 0, 0)),
                  pl.BlockSpec((1, tq, LANES), col(_FG)),
                  pl.BlockSpec((1, 1, S, 2), lambda b, h, i: (b, h, 0, 0))],
        out_specs=pl.BlockSpec((1, tq, LANES), lambda b, h, i: (b, i, h)),
        out_shape=jax.ShapeDtypeStruct((B, S, WIDTH), BF16),
        compiler_params=_params("arbitrary", "arbitrary", "arbitrary"),
        name="fox_attention",
    )(zm, zm, fvt, zm, f4)


def _sb_kernel(q_ref, k_ref, vt_ref, g_ref, sfx_ref, o_ref, *, tq, tk):
    qi = pl.program_id(2)
    n_off = qi * (tq // tk)
    qs_t = _pair_queries_t(q_ref[0], _head_masks())
    key_idx0 = lax.broadcasted_iota(I32, (tk, tq), 0)
    query_idx0 = lax.broadcasted_iota(I32, (tk, tq), 1)

    def step(c, carry, masked):
        later, acc = carry
        k = k_ref[0, pl.ds(pl.multiple_of(c * tk, tk), tk), :]
        z = jnp.dot(k, qs_t, preferred_element_type=F32)
        sp = jnp.maximum(z, 0.0) + jnp.log(1.0 + jnp.exp(-jnp.abs(z)))
        if masked:
            valid = (c - n_off) * tk + key_idx0 < query_idx0
            valid = jnp.concatenate([valid, valid], axis=1)
            sp = jnp.where(valid, sp, 0.0)
        hi = sp.astype(BF16)
        lo = (sp - hi.astype(F32)).astype(BF16)
        incl = jnp.dot(sfx_ref[...], jnp.concatenate([hi, lo], axis=0), preferred_element_type=F32)
        a = jnp.exp(z - incl - later)
        if masked:
            a = jnp.where(valid, a, 0.0)
        acc = acc + jnp.dot(vt_ref[0, 0, c], a.astype(BF16), preferred_element_type=F32)
        return later + incl[:1, :], acc

    carry = (jnp.zeros((1, 2 * tq), F32), jnp.zeros((LANES, 2 * tq), F32))
    for d in reversed(range(tq // tk)):
        carry = step(n_off + d, carry, True)
    carry = lax.fori_loop(0, n_off, lambda i, cr: step(n_off - 1 - i, cr, False), carry)
    o_ref[0] = _pair_output(carry[1], g_ref[0], tq).astype(o_ref.dtype)


def _sb(zm, svt, tq):
    B, S, _ = zm.shape
    tk = svt.shape[4]
    assert tq % tk == 0
    col = lambda base: (lambda b, h, i: (b, i, base * 4 + h))
    j = jnp.arange(2 * tk, dtype=I32)[None, :] % tk
    sfx = (j >= jnp.arange(tk, dtype=I32)[:, None]).astype(BF16)
    return pl.pallas_call(
        functools.partial(_sb_kernel, tq=tq, tk=tk),
        grid=(B, 4, S // tq),
        in_specs=[pl.BlockSpec((1, tq, LANES), col(_SQ)),
                  pl.BlockSpec((1, S, LANES), lambda b, h, i: (b, 0, _SK * 4 + h)),
                  pl.BlockSpec((1, 1, S // tk, LANES, tk), lambda b, h, i: (b, h, 0, 0, 0)),
                  pl.BlockSpec((1, tq, LANES), col(_SG)),
                  pl.BlockSpec((tk, 2 * tk), lambda b, h, i: (0, 0))],
        out_specs=pl.BlockSpec((1, tq, LANES), lambda b, h, i: (b, i, h)),
        out_shape=jax.ShapeDtypeStruct((B, S, WIDTH), BF16),
        compiler_params=_params("arbitrary", "arbitrary", "arbitrary"),
        name="stick_breaking_attention",
    )(zm, zm, svt, zm, sfx)


def _dsa_kernel(iq_ref, iqr_ref, misc_ref, cos_ref, sin_ref, dq_ref, dqr_ref, dg_ref,
                ikk_ref, kk_ref, vvt_ref, o_ref, key_ref, bias_ref, cut_ref, *, tq, tk, ktop, seq_bits):
    qi = pl.program_id(1)
    q0 = qi * tq
    nch = (q0 + tq + tk - 1) // tk
    masks = _head_masks()
    cos = cos_ref[0]
    sin = sin_ref[0]
    cos4 = jnp.concatenate([cos] * 4, axis=1)
    sin4 = jnp.concatenate([sin] * 4, axis=1)
    iq = iq_ref[0] * cos4 + iqr_ref[0] * sin4
    qd = dq_ref[0].astype(F32) * cos4 + dqr_ref[0].astype(F32) * sin4

    def slab_t(x, h):
        blk = x[:, (h // 2) * LANES:(h // 2 + 1) * LANES]
        return jnp.where(masks[h % 2], blk, 0.0).T.astype(BF16)

    iq_t = [jnp.concatenate([slab_t(iq, 2 * p), slab_t(iq, 2 * p + 1)], axis=1) for p in range(N_HEADS // 2)]
    qs_t = jnp.concatenate([slab_t(qd, h) for h in range(N_HEADS)], axis=1)
    w_t = misc_ref[0].T[N_HEADS:2 * N_HEADS, :] * (N_HEADS ** -0.5)
    w_pair = [jnp.concatenate([w_t[2 * p:2 * p + 1], w_t[2 * p + 1:2 * p + 2]], axis=1) for p in range(N_HEADS // 2)]

    key_idx0 = lax.broadcasted_iota(I32, (tk, tq), 0)
    query_idx = q0 + lax.broadcasted_iota(I32, (tk, tq), 1)

    def score_chunk(c, carry):
        ik = ikk_ref[0, pl.ds(pl.multiple_of(c * tk, tk), tk), :]
        acc2 = jnp.zeros((tk, 2 * tq), F32)
        for p in range(N_HEADS // 2):
            s = jnp.dot(ik, iq_t[p], preferred_element_type=F32)
            acc2 = acc2 + jnp.maximum(s, 0.0) * w_pair[p]
        acc = acc2[:, :tq] + acc2[:, tq:]
        acc = jnp.where(acc == 0.0, 0.0, acc)
        bits = lax.bitcast_convert_type(acc, I32)
        key = bits ^ (lax.shift_right_arithmetic(bits, 31) & 0x7FFFFFFF)
        key = jnp.maximum(key, INT_MIN + 1)
        key_ref[c] = jnp.where(c * tk + key_idx0 <= query_idx, key, INT_MIN)
        return carry

    lax.fori_loop(0, nch, score_chunk, 0)

    n_acc = 4

    def count(pred):
        def body(c, cnt):
            hit = jnp.where(pred(key_ref[c], c * tk + key_idx0), 1.0, 0.0)
            return cnt + hit.reshape(n_acc, tk // (8 * n_acc), 8, tq).sum(axis=1)
        cnt = lax.fori_loop(0, nch, body, jnp.zeros((n_acc, 8, tq), F32))
        return jnp.sum(cnt.reshape(n_acc * 8, tq), axis=0, keepdims=True)

    def bit_body(b, prefix):
        cand_u = prefix | lax.shift_left(jnp.int32(1), 31 - b)
        cand = jnp.broadcast_to(cand_u ^ INT_MIN, (tk, tq))
        n = count(lambda key, idx: key >= cand)
        return jnp.where(n >= ktop, cand_u, prefix)

    prefix = lax.fori_loop(0, 32, bit_body, jnp.zeros((1, tq), I32))
    thr = jnp.broadcast_to(prefix ^ INT_MIN, (tk, tq))
    n_gt = count(lambda key, idx: key > thr)
    n_ge = count(lambda key, idx: key >= thr)
    need = ktop - n_gt

    cut_ref[...] = jnp.full(cut_ref.shape, 2 ** seq_bits, I32)

    @pl.when(jnp.max(n_ge) > ktop)
    def _():
        def idx_body(b, lo):
            cand_i = lo | lax.shift_left(jnp.int32(1), seq_bits - 1 - b)
            cand = jnp.broadcast_to(cand_i, (tk, tq))
            g = count(lambda key, idx: (key == thr) & (idx < cand))
            return jnp.where(g < need, cand_i, lo)
        lo = lax.fori_loop(0, seq_bits, idx_body, jnp.zeros((1, tq), I32))
        cut_ref[...] = jnp.broadcast_to(lo, cut_ref.shape)

    cut = jnp.broadcast_to(cut_ref[:1, :], (tk, tq))

    def bias_chunk(c, carry):
        key = key_ref[c]
        sel = (key > thr) | ((key == thr) & (c * tk + key_idx0 <= cut))
        sel = sel & (key > INT_MIN)
        bias_ref[c] = jnp.where(sel, 0.0, NEG)
        return carry

    lax.fori_loop(0, nch, bias_chunk, 0)

    def att(c, carry):
        m, l, acc = carry
        start = pl.multiple_of(c * tk, tk)
        k = kk_ref[0, pl.ds(start, tk), :]
        v_t = vvt_ref[0, c]
        s = jnp.dot(k, qs_t, preferred_element_type=F32)
        s = s + jnp.concatenate([bias_ref[c]] * N_HEADS, axis=1)
        m_new = jnp.maximum(m, jnp.max(s, axis=0, keepdims=True))
        alpha = jnp.exp(m - m_new)
        pr = jnp.exp(s - m_new)
        l = alpha * l + jnp.sum(pr, axis=0, keepdims=True)
        acc = alpha * acc + jnp.dot(v_t, pr.astype(BF16), preferred_element_type=F32)
        return m_new, l, acc

    wide = N_HEADS * tq
    init = (jnp.full((1, wide), NEG, F32), jnp.zeros((1, wide), F32), jnp.zeros((LANES, wide), F32))
    _, l, acc = lax.fori_loop(0, nch, att, init)
    out_t = acc / l
    own = lax.broadcasted_iota(I32, (LANES, tq), 0) < HEAD_DIM
    for p in range(N_HEADS // 2):
        y_t = jnp.where(own, out_t[:, 2 * p * tq:(2 * p + 1) * tq], out_t[:, (2 * p + 1) * tq:(2 * p + 2) * tq])
        g = dg_ref[0, :, p * LANES:(p + 1) * LANES].astype(F32)
        o_ref[0, :, p * LANES:(p + 1) * LANES] = (y_t.T * _silu(g)).astype(o_ref.dtype)


def _dsa(zs, zm, cos, sin, ikk, kk, vvt, tq, ktop):
    B, S, _ = zs.shape
    tk = vvt.shape[3]
    assert tq == LANES
    wide = lambda col: pl.BlockSpec((1, tq, WIDTH), lambda b, i: (b, i, col))
    row = pl.BlockSpec((1, tq, LANES), lambda b, i: (b, i, 0))
    seq = pl.BlockSpec((1, S, LANES), lambda b, i: (b, 0, 0))
    seq_bits = max(1, (S - 1).bit_length())
    return pl.pallas_call(
        functools.partial(_dsa_kernel, tq=tq, tk=tk, ktop=ktop, seq_bits=seq_bits),
        grid=(B, S // tq),
        in_specs=[wide(_IQ), wide(_IQR),
                  pl.BlockSpec((1, tq, LANES), lambda b, i: (b, i, _MISC)),
                  row, row, wide(_DQ), wide(_DQR), wide(_DG), seq, seq,
                  pl.BlockSpec((1, S // tk, LANES, tk), lambda b, i: (b, 0, 0, 0))],
        out_specs=pl.BlockSpec((1, tq, WIDTH), lambda b, i: (b, i, 0)),
        out_shape=jax.ShapeDtypeStruct((B, S, WIDTH), BF16),
        scratch_shapes=[pltpu.VMEM((S // tk, tk, tq), I32),
                        pltpu.VMEM((S // tk, tk, tq), F32),
                        pltpu.VMEM((8, tq), I32)],
        compiler_params=_params("arbitrary", "arbitrary"),
        name="dsa_attention",
    )(zs, zs, zs, cos, sin, zm, zm, zm, ikk, kk, vvt)


def _merge_kernel(x_ref, gate_ref, yf_ref, yd_ref, ys_ref, m0_ref, m1_ref, m2_ref,
                  wf_ref, wd_ref, ws_ref, wo_ref, gfin_ref, o_ref, *, final):
    def branch(y_ref, w_ref, m_ref):
        proj = jnp.dot(y_ref[...], w_ref[...], preferred_element_type=F32)
        return jax.nn.sigmoid(m_ref[...].astype(F32)) * proj

    mixed = branch(yf_ref, wf_ref, m0_ref) + branch(yd_ref, wd_ref, m1_ref) + branch(ys_ref, ws_ref, m2_ref)
    o = x_ref[...] + gate_ref[0] * jnp.dot(mixed.astype(BF16), wo_ref[...], preferred_element_type=F32)
    if final:
        o = o * lax.rsqrt(jnp.mean(o * o, axis=-1, keepdims=True) + EPS) * gfin_ref[...]
    o_ref[...] = o


def _merge(x2, gate, yf, yd, ys, zm2, wf, wd, ws, wo, gfin, S, tm, final):
    M, D = x2.shape
    per_b = S // tm
    ysp = pl.BlockSpec((tm, WIDTH), lambda m: (m, 0))
    msp = lambda j: pl.BlockSpec((tm, D), lambda m: (m, j))
    wsp = pl.BlockSpec((WIDTH, D), lambda m: (0, 0))
    return pl.pallas_call(
        functools.partial(_merge_kernel, final=final),
        grid=(M // tm,),
        in_specs=[pl.BlockSpec((tm, D), lambda m: (m, 0)),
                  pl.BlockSpec((1, 1, D), lambda m: (m // per_b, 0, 0)),
                  ysp, ysp, ysp, msp(0), msp(1), msp(2), wsp, wsp, wsp,
                  pl.BlockSpec((D, D), lambda m: (0, 0)),
                  pl.BlockSpec((1, D), lambda m: (0, 0))],
        out_specs=pl.BlockSpec((tm, D), lambda m: (m, 0)),
        out_shape=jax.ShapeDtypeStruct((M, D), F32),
        compiler_params=_params("arbitrary"),
        name="merge_out",
    )(x2, gate, yf, yd, ys, zm2, zm2, zm2, wf, wd, ws, wo, gfin)


def _rot_cols(w):
    d, n = w.shape
    w4 = w.reshape(d, n // HEAD_DIM, 2, HEAD_DIM // 2)
    return jnp.concatenate([-w4[:, :, 1:], w4[:, :, :1]], axis=2).reshape(d, n)


def _split_w_in(w):
    sizes = (WIDTH, WIDTH, WIDTH, N_HEADS, WIDTH,
             WIDTH, KV_RANK, WIDTH, HEAD_DIM, N_HEADS, WIDTH,
             WIDTH, WIDTH, WIDTH, WIDTH, w.shape[1] - (11 * WIDTH + 2 * N_HEADS + KV_RANK + HEAD_DIM))
    parts, off = [], 0
    for s in sizes:
        parts.append(w[:, off:off + s])
        off += s
    return parts


def _layer_weights(w_in, w_kv_up):
    (fq, fk, fv, ff, fg, dq, dckv, diq, dik, diw, dg, sq, sk, sv, sg, merge) = _split_w_in(w_in)
    scale = HEAD_DIM ** -0.5
    fq, dq, sq, diq = fq * scale, dq * scale, sq * scale, diq * scale
    w_main = jnp.concatenate([merge, fq, fk, fv, fg, dq, _rot_cols(dq), dg, sq, sk, sv, sg], axis=1)
    pad = jnp.zeros((w_in.shape[0], LANES - 2 * N_HEADS), w_in.dtype)
    dikr = _rot_cols(dik)
    w_small = jnp.concatenate([diq, _rot_cols(diq), ff, diw, pad, dckv, dik, dik, dikr, dikr], axis=1)
    wk, wv = w_kv_up[:, :HEAD_DIM], w_kv_up[:, HEAD_DIM:]
    wkr = _rot_cols(wk)
    w_kv = jnp.concatenate([wk, wk, wkr, wkr, wv, wv], axis=1)
    return w_main.astype(BF16), w_small.astype(BF16), w_kv.astype(BF16)


def kernel(x, c, positions, w_ada, b_ada, g_norm, w_in, b_fgt, g_kv, w_kv_up, w_br_fox, w_br_dsa, w_br_sb, w_out, g_final):
    B, S, D = x.shape
    depth = w_in.shape[0]
    assert D == 2 * WIDTH and S % 256 == 0
    ktop = min(TOPK_MAX, S // 4)
    tm = min(1024, S)
    tp = min(512, S)
    tq_att = min(512, S)
    tk_dsa = tk_fox = tp
    tk_sb = 256

    half = HEAD_DIM // 2
    inv_freq = ROPE_THETA ** (-jnp.arange(half, dtype=F32) / half)
    ang = positions.astype(F32)[..., None] * inv_freq
    cos = jnp.tile(jnp.cos(ang), (1, 1, LANES // half))
    sin = jnp.tile(jnp.sin(ang), (1, 1, LANES // half))

    mod = _modulation(c, w_ada, b_ada)
    x2 = x.reshape(B * S, D)
    for l in range(depth):
        shift = mod[l, :, None, :D]
        scale = mod[l, :, None, D:2 * D]
        gate = mod[l, :, None, 2 * D:]
        w_main, w_small, w_kv = _layer_weights(w_in[l], w_kv_up[l])
        g = g_norm[l].reshape(1, D)
        zm2 = _norm_proj(x2, g, scale, shift, w_main, S, BF16, tm, 512, "norm_proj_main")
        zs2 = _norm_proj(x2, g, scale, shift, w_small, S, F32, tm, 512, "norm_proj_small")
        zm = zm2.reshape(B, S, N_MAIN)
        zs = zs2.reshape(B, S, N_SMALL)
        bf = jnp.zeros((1, LANES), F32).at[0, :N_HEADS].set(b_fgt[l])
        fcol, kk, vvt, ikk, fvt, svt = _prep(zs, zm, cos, sin, bf, g_kv[l].reshape(1, KV_RANK), w_kv,
                                             tp, tk_dsa, tk_fox, tk_sb)
        y_fox = _fox(zm, fcol, fvt, tq_att)
        y_sb = _sb(zm, svt, tq_att)
        y_dsa = _dsa(zs, zm, cos, sin, ikk, kk, vvt, LANES, ktop)
        x2 = _merge(x2, gate, y_fox.reshape(B * S, WIDTH), y_dsa.reshape(B * S, WIDTH),
                    y_sb.reshape(B * S, WIDTH), zm2,
                    w_br_fox[l].astype(BF16), w_br_dsa[l].astype(BF16), w_br_sb[l].astype(BF16),
                    w_out[l].astype(BF16), g_final.reshape(1, D), S, min(512, S), l == depth - 1)
    return x2.reshape(B, S, D)
```

```python
import functools

import jax
import jax.numpy as jnp
from jax import lax
from jax.experimental import pallas as pl
from jax.experimental.pallas import tpu as pltpu

F32 = jnp.float32
BF16 = jnp.bfloat16
I32 = jnp.int32
HIGHEST = lax.Precision.HIGHEST

HEAD_DIM = 64
N_HEADS = 8
WIDTH = N_HEADS * HEAD_DIM
KV_RANK = 128
TOPK_MAX = 256
ROPE_THETA = 10000.0
EPS = 1e-6
LANES = 128
NEG = -1e30
INT_MIN = -(2 ** 31)
VMEM_LIMIT = 56 * 1024 * 1024

_MERGE, _FQ, _FK, _FV, _FG, _DQ, _DQR, _DG, _SQ, _SK, _SV, _SG = 0, 6, 7, 8, 9, 10, 11, 12, 13, 14, 15, 16
N_MAIN = 17 * WIDTH
_IQ, _IQR = 0, 1
_MISC, _CKV, _IK, _IKR = 8, 9, 10, 11
N_SMALL = 12 * LANES


def _params(*sem):
    return pltpu.CompilerParams(dimension_semantics=sem, vmem_limit_bytes=VMEM_LIMIT)


def _mod_kernel(c_ref, w_ref, b_ref, o_ref):
    c = c_ref[...]
    sc = c * jax.nn.sigmoid(c)
    o_ref[0] = jnp.dot(sc, w_ref[0], preferred_element_type=F32, precision=HIGHEST) + b_ref[0]


def _modulation(c, w_ada, b_ada):
    L, D, N = w_ada.shape
    B = c.shape[0]
    tn = 512
    return pl.pallas_call(
        _mod_kernel,
        grid=(L, N // tn),
        in_specs=[pl.BlockSpec((B, D), lambda l, n: (0, 0)),
                  pl.BlockSpec((1, D, tn), lambda l, n: (l, 0, n)),
                  pl.BlockSpec((1, 1, tn), lambda l, n: (l, 0, n))],
        out_specs=pl.BlockSpec((1, B, tn), lambda l, n: (l, 0, n)),
        out_shape=jax.ShapeDtypeStruct((L, B, N), F32),
        compiler_params=_params("arbitrary", "arbitrary"),
        name="adaln_mod",
    )(c, w_ada, b_ada.reshape(L, 1, N))


def _norm_proj_kernel(x_ref, g_ref, sc_ref, sh_ref, w_ref, o_ref, h_ref):
    @pl.when(pl.program_id(1) == 0)
    def _():
        x = x_ref[...]
        y = x * lax.rsqrt(jnp.mean(x * x, axis=-1, keepdims=True) + EPS)
        h = (y * g_ref[...]) * (1.0 + sc_ref[0]) + sh_ref[0]
        h_ref[...] = h.astype(BF16)

    o_ref[...] = jnp.dot(h_ref[...], w_ref[...], preferred_element_type=F32).astype(o_ref.dtype)


def _norm_proj(x2, g, scale, shift, w, S, out_dtype, tm, tn, name):
    M, D = x2.shape
    N = w.shape[1]
    per_b = S // tm
    return pl.pallas_call(
        _norm_proj_kernel,
        grid=(M // tm, N // tn),
        in_specs=[pl.BlockSpec((tm, D), lambda m, n: (m, 0)),
                  pl.BlockSpec((1, D), lambda m, n: (0, 0)),
                  pl.BlockSpec((1, 1, D), lambda m, n: (m // per_b, 0, 0)),
                  pl.BlockSpec((1, 1, D), lambda m, n: (m // per_b, 0, 0)),
                  pl.BlockSpec((D, tn), lambda m, n: (0, n))],
        out_specs=pl.BlockSpec((tm, tn), lambda m, n: (m, n)),
        out_shape=jax.ShapeDtypeStruct((M, N), out_dtype),
        scratch_shapes=[pltpu.VMEM((tm, D), BF16)],
        compiler_params=_params("arbitrary", "arbitrary"),
        name=name,
    )(x2, g, scale, shift, w)


def _store_transposed(x, out_ref, lead=()):
    x_t = x.astype(F32).T.astype(out_ref.dtype)
    chunk = out_ref.shape[-1]
    for i in range(x.shape[0] // chunk):
        out_ref[lead + (i,)] = x_t[:, i * chunk:(i + 1) * chunk]


def _prep_kernel(misc_ref, ckv_ref, ik_ref, ikr_ref, cos_ref, sin_ref, bf_ref, gkv_ref, wkv_ref, fv_ref, sv_ref,
                 fcol_ref, kk_ref, vvt_ref, ikk_ref, fvt_ref, svt_ref, carry_ref):
    @pl.when(pl.program_id(1) == 0)
    def _():
        carry_ref[...] = jnp.zeros_like(carry_ref)

    for p in range(N_HEADS // 2):
        _store_transposed(fv_ref[0, :, p * LANES:(p + 1) * LANES], fvt_ref, (0, p))
        _store_transposed(sv_ref[0, :, p * LANES:(p + 1) * LANES], svt_ref, (0, p))

    tp = misc_ref.shape[1]
    xf = misc_ref[0] + bf_ref[...]
    logf = jnp.minimum(xf, 0.0) - jnp.log1p(jnp.exp(-jnp.abs(xf)))
    r = lax.broadcasted_iota(I32, (tp, tp), 0)
    c = lax.broadcasted_iota(I32, (tp, tp), 1)
    tri = jnp.where(r >= c, 1.0, 0.0).astype(F32)
    cum = jnp.dot(tri, logf, preferred_element_type=F32, precision=HIGHEST) + carry_ref[...]
    carry_ref[...] = cum[tp - 1:tp, :]
    fcol_ref[0] = cum[:, :N_HEADS]

    cos = cos_ref[0]
    sin = sin_ref[0]
    ckv = ckv_ref[0]
    nrm = ckv * lax.rsqrt(jnp.mean(ckv * ckv, axis=-1, keepdims=True) + EPS) * gkv_ref[...]
    kv = jnp.dot(nrm.astype(BF16), wkv_ref[...], preferred_element_type=F32)
    kk_ref[0] = (kv[:, :LANES] * cos + kv[:, LANES:2 * LANES] * sin).astype(BF16)
    _store_transposed(kv[:, 2 * LANES:], vvt_ref, (0,))
    ikk_ref[0] = (ik_ref[0] * cos + ikr_ref[0] * sin).astype(BF16)


def _prep(zs, zm, cos, sin, bf, gkv, wkv, tp, tk_dsa, tk_fox, tk_sb):
    B, S, _ = zs.shape
    assert tp % tk_dsa == 0 and tp % tk_fox == 0 and tp % tk_sb == 0
    blk = lambda col: pl.BlockSpec((1, tp, LANES), lambda b, j: (b, j, col))
    row = pl.BlockSpec((1, tp, LANES), lambda b, j: (b, j, 0))
    wide = lambda col: pl.BlockSpec((1, tp, WIDTH), lambda b, j: (b, j, col))
    vt_spec = lambda tk: pl.BlockSpec((1, N_HEADS // 2, tp // tk, LANES, tk), lambda b, j: (b, 0, j, 0, 0))
    vt_shape = lambda tk: jax.ShapeDtypeStruct((B, N_HEADS // 2, S // tk, LANES, tk), BF16)
    return pl.pallas_call(
        _prep_kernel,
        grid=(B, S // tp),
        in_specs=[blk(_MISC), blk(_CKV), blk(_IK), blk(_IKR), row, row,
                  pl.BlockSpec((1, LANES), lambda b, j: (0, 0)),
                  pl.BlockSpec((1, LANES), lambda b, j: (0, 0)),
                  pl.BlockSpec((KV_RANK, 3 * LANES), lambda b, j: (0, 0)),
                  wide(_FV), wide(_SV)],
        out_specs=[pl.BlockSpec((1, tp, N_HEADS), lambda b, j: (b, j, 0)),
                   row, pl.BlockSpec((1, tp // tk_dsa, LANES, tk_dsa), lambda b, j: (b, j, 0, 0)), row,
                   vt_spec(tk_fox), vt_spec(tk_sb)],
        out_shape=[jax.ShapeDtypeStruct((B, S, N_HEADS), F32),
                   jax.ShapeDtypeStruct((B, S, LANES), BF16),
                   jax.ShapeDtypeStruct((B, S // tk_dsa, LANES, tk_dsa), BF16),
                   jax.ShapeDtypeStruct((B, S, LANES), BF16),
                   vt_shape(tk_fox), vt_shape(tk_sb)],
        scratch_shapes=[pltpu.VMEM((1, LANES), F32)],
        compiler_params=_params("arbitrary", "arbitrary"),
        name="prep",
    )(zs, zs, zs, zs, cos, sin, bf, gkv, wkv, zm, zm)


def _head_masks():
    lane = lax.broadcasted_iota(I32, (1, LANES), 1)
    lo = lane < HEAD_DIM
    return lo, jnp.logical_not(lo)


def _silu(g):
    return g * jax.nn.sigmoid(g)


def _pair_queries_t(q, masks):
    q = q.astype(F32)
    return jnp.concatenate([jnp.where(masks[hh], q, 0.0).T.astype(BF16) for hh in range(2)], axis=1)


def _pair_output(out_t, g, tq):
    own = lax.broadcasted_iota(I32, (LANES, tq), 0) < HEAD_DIM
    return jnp.where(own, out_t[:, :tq], out_t[:, tq:]).T * _silu(g.astype(F32))


def _fox_kernel(q_ref, k_ref, vt_ref, g_ref, fcol_ref, o_ref, s_ref, *, t):
    qi = pl.program_id(2)
    qs_t = _pair_queries_t(q_ref[0], _head_masks())
    key_idx0 = lax.broadcasted_iota(I32, (t, t), 0)
    query_idx0 = lax.broadcasted_iota(I32, (t, t), 1)
    f0 = fcol_ref[0, 0, pl.ds(pl.multiple_of(qi * t, t), 1), :]

    def scores(j, slot):
        start = pl.multiple_of(j * t, t)
        b = f0 - fcol_ref[0, 0, pl.ds(start, t), :]
        bias = jnp.concatenate([jnp.broadcast_to(b[:, hh:hh + 1], (t, t)) for hh in range(2)], axis=1)
        s_ref[slot] = jnp.dot(k_ref[0, pl.ds(start, t), :], qs_t, preferred_element_type=F32) + bias

    def consume(j, slot, carry, diagonal):
        m, l, acc = carry
        s = s_ref[slot]
        if diagonal:
            valid = key_idx0 <= query_idx0
            s = jnp.where(jnp.concatenate([valid, valid], axis=1), s, NEG)
        m_new = jnp.maximum(m, jnp.max(s, axis=0, keepdims=True))
        alpha = jnp.exp(m - m_new)
        p = jnp.exp(s - m_new)
        l = alpha * l + jnp.sum(p, axis=0, keepdims=True)
        acc = alpha * acc + jnp.dot(vt_ref[0, 0, j], p.astype(BF16), preferred_element_type=F32)
        return m_new, l, acc

    def pair(i, carry):
        scores(2 * i + 1, 1)
        carry = consume(2 * i, 0, carry, False)
        scores(2 * i + 2, 0)
        return consume(2 * i + 1, 1, carry, False)

    def even_tail(carry):
        return consume(qi, 0, carry, True)

    def odd_tail(carry):
        scores(qi, 1)
        carry = consume(qi - 1, 0, carry, False)
        return consume(qi, 1, carry, True)

    init = (jnp.full((1, 2 * t), NEG, F32), jnp.zeros((1, 2 * t), F32), jnp.zeros((LANES, 2 * t), F32))
    scores(0, 0)
    carry = lax.fori_loop(0, qi // 2, pair, init)
    _, l, acc = lax.cond(qi % 2 == 1, odd_tail, even_tail, carry)
    o_ref[0] = _pair_output(acc / l, g_ref[0], t).astype(o_ref.dtype)


def _fox(zm, fcol, fvt, tq):
    B, S, _ = zm.shape
    tk = fvt.shape[4]
    assert tq == tk
    col = lambda base: (lambda b, h, i: (b, i, base * 4 + h))
    f4 = fcol.reshape(B, S, 4, 2).transpose(0, 2, 1, 3)
    return pl.pallas_call(
        functools.partial(_fox_kernel, t=tq),
        grid=(B, 4, S // tq),
        in_specs=[pl.BlockSpec((1, tq, LANES), col(_FQ)),
                  pl.BlockSpec((1, S, LANES), lambda b, h, i: (b, 0, _FK * 4 + h)),
                  pl.BlockSpec((1, 1, S // tk, LANES, tk), lambda b, h, i: (b, h, 0, 0, 0)),
                  pl.BlockSpec((1, tq, LANES), col(_FG)),
                  pl.BlockSpec((1, 1, S, 2), lambda b, h, i: (b, h, 0, 0))],
        out_specs=pl.BlockSpec((1, tq, LANES), lambda b, h, i: (b, i, h)),
        out_shape=jax.ShapeDtypeStruct((B, S, WIDTH), BF16),
        scratch_shapes=[pltpu.VMEM((2, tk, 2 * tq), F32)],
        compiler_params=_params("arbitrary", "arbitrary", "arbitrary"),
        name="fox_attention",
    )(zm, zm, fvt, zm, f4)


def _sb_kernel(q_ref, k_ref, vt_ref, g_ref, sfx_ref, o_ref, z_ref, *, tq, tk):
    assert tq == 2 * tk
    qi = pl.program_id(2)
    n_off = 2 * qi
    qs_t = _pair_queries_t(q_ref[0], _head_masks())
    key_idx0 = lax.broadcasted_iota(I32, (tk, tq), 0)
    query_idx0 = lax.broadcasted_iota(I32, (tk, tq), 1)

    def logits(u, slot):
        c = jnp.maximum(n_off + 1 - u, 0)
        z_ref[slot] = jnp.dot(k_ref[0, pl.ds(pl.multiple_of(c * tk, tk), tk), :], qs_t, preferred_element_type=F32)

    def consume(u, slot, carry, masked):
        later, acc = carry
        c = n_off + 1 - u
        z = z_ref[slot]
        sp = jnp.maximum(z, 0.0) + jnp.log(1.0 + jnp.exp(-jnp.abs(z)))
        if masked:
            valid = (c - n_off) * tk + key_idx0 < query_idx0
            valid = jnp.concatenate([valid, valid], axis=1)
            sp = jnp.where(valid, sp, 0.0)
        hi = sp.astype(BF16)
        lo = (sp - hi.astype(F32)).astype(BF16)
        incl = jnp.dot(sfx_ref[...], jnp.concatenate([hi, lo], axis=0), preferred_element_type=F32)
        a = jnp.exp(z - incl - later)
        if masked:
            a = jnp.where(valid, a, 0.0)
        acc = acc + jnp.dot(vt_ref[0, 0, c], a.astype(BF16), preferred_element_type=F32)
        return later + incl[:1, :], acc

    def pair(i, carry, masked):
        logits(2 * i + 1, 1)
        carry = consume(2 * i, 0, carry, masked)
        logits(2 * i + 2, 0)
        return consume(2 * i + 1, 1, carry, masked)

    carry = (jnp.zeros((1, 2 * tq), F32), jnp.zeros((LANES, 2 * tq), F32))
    logits(0, 0)
    carry = pair(0, carry, True)
    carry = lax.fori_loop(1, qi + 1, functools.partial(pair, masked=False), carry)
    o_ref[0] = _pair_output(carry[1], g_ref[0], tq).astype(o_ref.dtype)


def _sb(zm, svt, tq):
    B, S, _ = zm.shape
    tk = svt.shape[4]
    assert tq % tk == 0
    col = lambda base: (lambda b, h, i: (b, i, base * 4 + h))
    j = jnp.arange(2 * tk, dtype=I32)[None, :] % tk
    sfx = (j >= jnp.arange(tk, dtype=I32)[:, None]).astype(BF16)
    return pl.pallas_call(
        functools.partial(_sb_kernel, tq=tq, tk=tk),
        grid=(B, 4, S // tq),
        in_specs=[pl.BlockSpec((1, tq, LANES), col(_SQ)),
                  pl.BlockSpec((1, S, LANES), lambda b, h, i: (b, 0, _SK * 4 + h)),
                  pl.BlockSpec((1, 1, S // tk, LANES, tk), lambda b, h, i: (b, h, 0, 0, 0)),
                  pl.BlockSpec((1, tq, LANES), col(_SG)),
                  pl.BlockSpec((tk, 2 * tk), lambda b, h, i: (0, 0))],
        out_specs=pl.BlockSpec((1, tq, LANES), lambda b, h, i: (b, i, h)),
        out_shape=jax.ShapeDtypeStruct((B, S, WIDTH), BF16),
        scratch_shapes=[pltpu.VMEM((2, tk, 2 * tq), F32)],
        compiler_params=_params("arbitrary", "arbitrary", "arbitrary"),
        name="stick_breaking_attention",
    )(zm, zm, svt, zm, sfx)


def _dsa_kernel(iq_ref, iqr_ref, misc_ref, cos_ref, sin_ref, dq_ref, dqr_ref, dg_ref,
                ikk_ref, kk_ref, vvt_ref, o_ref, key_ref, bias_ref, cut_ref, s_ref, *, tq, tk, ktop, seq_bits):
    qi = pl.program_id(1)
    q0 = qi * tq
    nch = (q0 + tq + tk - 1) // tk
    masks = _head_masks()
    cos = cos_ref[0]
    sin = sin_ref[0]
    cos4 = jnp.concatenate([cos] * 4, axis=1)
    sin4 = jnp.concatenate([sin] * 4, axis=1)
    iq = iq_ref[0] * cos4 + iqr_ref[0] * sin4
    qd = dq_ref[0].astype(F32) * cos4 + dqr_ref[0].astype(F32) * sin4

    def slab_t(x, h):
        blk = x[:, (h // 2) * LANES:(h // 2 + 1) * LANES]
        return jnp.where(masks[h % 2], blk, 0.0).T.astype(BF16)

    iq_t = [jnp.concatenate([slab_t(iq, 2 * p), slab_t(iq, 2 * p + 1)], axis=1) for p in range(N_HEADS // 2)]
    qs_t = jnp.concatenate([slab_t(qd, h) for h in range(N_HEADS)], axis=1)
    w_t = misc_ref[0].T[N_HEADS:2 * N_HEADS, :] * (N_HEADS ** -0.5)
    w_pair = [jnp.concatenate([w_t[2 * p:2 * p + 1], w_t[2 * p + 1:2 * p + 2]], axis=1) for p in range(N_HEADS // 2)]

    key_idx0 = lax.broadcasted_iota(I32, (tk, tq), 0)
    query_idx = q0 + lax.broadcasted_iota(I32, (tk, tq), 1)

    def score_chunk(c, carry):
        ik = ikk_ref[0, pl.ds(pl.multiple_of(c * tk, tk), tk), :]
        acc2 = jnp.zeros((tk, 2 * tq), F32)
        for p in range(N_HEADS // 2):
            s = jnp.dot(ik, iq_t[p], preferred_element_type=F32)
            acc2 = acc2 + jnp.maximum(s, 0.0) * w_pair[p]
        acc = acc2[:, :tq] + acc2[:, tq:]
        acc = jnp.where(acc == 0.0, 0.0, acc)
        bits = lax.bitcast_convert_type(acc, I32)
        key = bits ^ (lax.shift_right_arithmetic(bits, 31) & 0x7FFFFFFF)
        key = jnp.maximum(key, INT_MIN + 1)
        key_ref[c] = jnp.where(c * tk + key_idx0 <= query_idx, key, INT_MIN)
        return carry

    lax.fori_loop(0, nch, score_chunk, 0)

    n_acc = 4

    slab = lambda row: jnp.broadcast_to(row, (tk, tq))

    def count(pred):
        def body(c, cnt):
            hit = jnp.where(pred(key_ref[c], c * tk + key_idx0), 1.0, 0.0)
            return cnt + hit.reshape(n_acc, tk // (8 * n_acc), 8, tq).sum(axis=1)
        cnt = lax.fori_loop(0, nch, body, jnp.zeros((n_acc, 8, tq), F32))
        return jnp.sum(cnt.reshape(n_acc * 8, tq), axis=0, keepdims=True)

    def bit_body(b, prefix):
        cand_u = prefix | lax.shift_left(jnp.int32(1), 31 - b)
        cand = slab(cand_u ^ INT_MIN)
        n = count(lambda key, idx: key >= cand)
        return jnp.where(n >= ktop, cand_u, prefix)

    prefix = lax.fori_loop(0, 32, bit_body, jnp.zeros((1, tq), I32))
    thr_s = slab(prefix ^ INT_MIN)
    n_gt = count(lambda key, idx: key > thr_s)
    n_ge = count(lambda key, idx: key >= thr_s)
    need = ktop - n_gt

    cut_ref[...] = jnp.full(cut_ref.shape, 2 ** seq_bits, I32)

    @pl.when(jnp.max(n_ge) > ktop)
    def _():
        def idx_body(b, lo):
            cand_i = lo | lax.shift_left(jnp.int32(1), seq_bits - 1 - b)
            cand = slab(cand_i)
            g = count(lambda key, idx: (key == thr_s) & (idx < cand))
            return jnp.where(g < need, cand_i, lo)
        lo = lax.fori_loop(0, seq_bits, idx_body, jnp.zeros((1, tq), I32))
        cut_ref[...] = jnp.broadcast_to(lo, cut_ref.shape)

    thr = thr_s
    cut = slab(cut_ref[:1, :])

    def bias_chunk(c, carry):
        key = key_ref[c]
        sel = (key > thr) | ((key == thr) & (c * tk + key_idx0 <= cut))
        sel = sel & (key > INT_MIN)
        bias_ref[c] = jnp.where(sel, 0.0, NEG)
        return carry

    lax.fori_loop(0, nch, bias_chunk, 0)

    def logits(c, slot):
        k = kk_ref[0, pl.ds(pl.multiple_of(c * tk, tk), tk), :]
        s = jnp.dot(k, qs_t, preferred_element_type=F32)
        s_ref[slot] = s + jnp.concatenate([bias_ref[c]] * N_HEADS, axis=1)

    def consume(c, slot, carry):
        m, l, acc = carry
        s = s_ref[slot]
        m_new = jnp.maximum(m, jnp.max(s, axis=0, keepdims=True))
        alpha = jnp.exp(m - m_new)
        pr = jnp.exp(s - m_new)
        l = alpha * l + jnp.sum(pr, axis=0, keepdims=True)
        acc = alpha * acc + jnp.dot(vvt_ref[0, c], pr.astype(BF16), preferred_element_type=F32)
        return m_new, l, acc

    def pair(i, carry):
        logits(2 * i + 1, 1)
        carry = consume(2 * i, 0, carry)
        logits(2 * i + 2, 0)
        return consume(2 * i + 1, 1, carry)

    def odd_tail(carry):
        return consume(nch - 1, 0, carry)

    def even_tail(carry):
        logits(nch - 1, 1)
        carry = consume(nch - 2, 0, carry)
        return consume(nch - 1, 1, carry)

    wide = N_HEADS * tq
    init = (jnp.full((1, wide), NEG, F32), jnp.zeros((1, wide), F32), jnp.zeros((LANES, wide), F32))
    logits(0, 0)
    carry = lax.fori_loop(0, (nch - 1) // 2, pair, init)
    _, l, acc = lax.cond(nch % 2 == 1, odd_tail, even_tail, carry)
    out_t = acc / l
    own = lax.broadcasted_iota(I32, (LANES, tq), 0) < HEAD_DIM
    for p in range(N_HEADS // 2):
        y_t = jnp.where(own, out_t[:, 2 * p * tq:(2 * p + 1) * tq], out_t[:, (2 * p + 1) * tq:(2 * p + 2) * tq])
        g = dg_ref[0, :, p * LANES:(p + 1) * LANES].astype(F32)
        o_ref[0, :, p * LANES:(p + 1) * LANES] = (y_t.T * _silu(g)).astype(o_ref.dtype)


def _dsa(zs, zm, cos, sin, ikk, kk, vvt, tq, ktop):
    B, S, _ = zs.shape
    tk = vvt.shape[3]
    assert tq % LANES == 0
    wide = lambda col: pl.BlockSpec((1, tq, WIDTH), lambda b, i: (b, i, col))
    row = pl.BlockSpec((1, tq, LANES), lambda b, i: (b, i, 0))
    seq = pl.BlockSpec((1, S, LANES), lambda b, i: (b, 0, 0))
    seq_bits = max(1, (S - 1).bit_length())
    return pl.pallas_call(
        functools.partial(_dsa_kernel, tq=tq, tk=tk, ktop=ktop, seq_bits=seq_bits),
        grid=(B, S // tq),
        in_specs=[wide(_IQ), wide(_IQR),
                  pl.BlockSpec((1, tq, LANES), lambda b, i: (b, i, _MISC)),
                  row, row, wide(_DQ), wide(_DQR), wide(_DG), seq, seq,
                  pl.BlockSpec((1, S // tk, LANES, tk), lambda b, i: (b, 0, 0, 0))],
        out_specs=pl.BlockSpec((1, tq, WIDTH), lambda b, i: (b, i, 0)),
        out_shape=jax.ShapeDtypeStruct((B, S, WIDTH), BF16),
        scratch_shapes=[pltpu.VMEM((S // tk, tk, tq), I32),
                        pltpu.VMEM((S // tk, tk, tq), F32),
                        pltpu.VMEM((8, tq), I32),
                        pltpu.VMEM((2, tk, N_HEADS * tq), F32)],
        compiler_params=_params("arbitrary", "arbitrary"),
        name="dsa_attention",
    )(zs, zs, zs, cos, sin, zm, zm, zm, ikk, kk, vvt)


def _merge_kernel(x_ref, gate_ref, yf_ref, yd_ref, ys_ref, m0_ref, m1_ref, m2_ref,
                  wf_ref, wd_ref, ws_ref, wo_ref, gfin_ref, o_ref, *, final):
    def branch(y_ref, w_ref, m_ref):
        proj = jnp.dot(y_ref[...], w_ref[...], preferred_element_type=F32)
        return jax.nn.sigmoid(m_ref[...].astype(F32)) * proj

    mixed = branch(yf_ref, wf_ref, m0_ref) + branch(yd_ref, wd_ref, m1_ref) + branch(ys_ref, ws_ref, m2_ref)
    o = x_ref[...] + gate_ref[0] * jnp.dot(mixed.astype(BF16), wo_ref[...], preferred_element_type=F32)
    if final:
        o = o * lax.rsqrt(jnp.mean(o * o, axis=-1, keepdims=True) + EPS) * gfin_ref[...]
    o_ref[...] = o


def _merge(x2, gate, yf, yd, ys, zm2, wf, wd, ws, wo, gfin, S, tm, final):
    M, D = x2.shape
    per_b = S // tm
    ysp = pl.BlockSpec((tm, WIDTH), lambda m: (m, 0))
    msp = lambda j: pl.BlockSpec((tm, D), lambda m: (m, j))
    wsp = pl.BlockSpec((WIDTH, D), lambda m: (0, 0))
    return pl.pallas_call(
        functools.partial(_merge_kernel, final=final),
        grid=(M // tm,),
        in_specs=[pl.BlockSpec((tm, D), lambda m: (m, 0)),
                  pl.BlockSpec((1, 1, D), lambda m: (m // per_b, 0, 0)),
                  ysp, ysp, ysp, msp(0), msp(1), msp(2), wsp, wsp, wsp,
                  pl.BlockSpec((D, D), lambda m: (0, 0)),
                  pl.BlockSpec((1, D), lambda m: (0, 0))],
        out_specs=pl.BlockSpec((tm, D), lambda m: (m, 0)),
        out_shape=jax.ShapeDtypeStruct((M, D), F32),
        compiler_params=_params("arbitrary"),
        name="merge_out",
    )(x2, gate, yf, yd, ys, zm2, zm2, zm2, wf, wd, ws, wo, gfin)


def _rot_cols(w):
    d, n = w.shape
    w4 = w.reshape(d, n // HEAD_DIM, 2, HEAD_DIM // 2)
    return jnp.concatenate([-w4[:, :, 1:], w4[:, :, :1]], axis=2).reshape(d, n)


def _split_w_in(w):
    sizes = (WIDTH, WIDTH, WIDTH, N_HEADS, WIDTH,
             WIDTH, KV_RANK, WIDTH, HEAD_DIM, N_HEADS, WIDTH,
             WIDTH, WIDTH, WIDTH, WIDTH, w.shape[1] - (11 * WIDTH + 2 * N_HEADS + KV_RANK + HEAD_DIM))
    parts, off = [], 0
    for s in sizes:
        parts.append(w[:, off:off + s])
        off += s
    return parts


def _layer_weights(w_in, w_kv_up):
    (fq, fk, fv, ff, fg, dq, dckv, diq, dik, diw, dg, sq, sk, sv, sg, merge) = _split_w_in(w_in)
    scale = HEAD_DIM ** -0.5
    fq, dq, sq, diq = fq * scale, dq * scale, sq * scale, diq * scale
    w_main = jnp.concatenate([merge, fq, fk, fv, fg, dq, _rot_cols(dq), dg, sq, sk, sv, sg], axis=1)
    pad = jnp.zeros((w_in.shape[0], LANES - 2 * N_HEADS), w_in.dtype)
    dikr = _rot_cols(dik)
    w_small = jnp.concatenate([diq, _rot_cols(diq), ff, diw, pad, dckv, dik, dik, dikr, dikr], axis=1)
    wk, wv = w_kv_up[:, :HEAD_DIM], w_kv_up[:, HEAD_DIM:]
    wkr = _rot_cols(wk)
    w_kv = jnp.concatenate([wk, wk, wkr, wkr, wv, wv], axis=1)
    return w_main.astype(BF16), w_small.astype(BF16), w_kv.astype(BF16)


def kernel(x, c, positions, w_ada, b_ada, g_norm, w_in, b_fgt, g_kv, w_kv_up, w_br_fox, w_br_dsa, w_br_sb, w_out, g_final):
    B, S, D = x.shape
    depth = w_in.shape[0]
    assert D == 2 * WIDTH and S % 256 == 0
    ktop = min(TOPK_MAX, S // 4)
    tm = min(1024, S)
    tp = min(512, S)
    tq_att = min(512, S)
    tk_dsa = tk_fox = tp
    tk_sb = 256

    half = HEAD_DIM // 2
    inv_freq = ROPE_THETA ** (-jnp.arange(half, dtype=F32) / half)
    ang = positions.astype(F32)[..., None] * inv_freq
    cos = jnp.tile(jnp.cos(ang), (1, 1, LANES // half))
    sin = jnp.tile(jnp.sin(ang), (1, 1, LANES // half))

    mod = _modulation(c, w_ada, b_ada)
    x2 = x.reshape(B * S, D)
    for l in range(depth):
        shift = mod[l, :, None, :D]
        scale = mod[l, :, None, D:2 * D]
        gate = mod[l, :, None, 2 * D:]
        w_main, w_small, w_kv = _layer_weights(w_in[l], w_kv_up[l])
        g = g_norm[l].reshape(1, D)
        zm2 = _norm_proj(x2, g, scale, shift, w_main, S, BF16, tm, 512, "norm_proj_main")
        zs2 = _norm_proj(x2, g, scale, shift, w_small, S, F32, tm, 512, "norm_proj_small")
        zm = zm2.reshape(B, S, N_MAIN)
        zs = zs2.reshape(B, S, N_SMALL)
        bf = jnp.zeros((1, LANES), F32).at[0, :N_HEADS].set(b_fgt[l])
        fcol, kk, vvt, ikk, fvt, svt = _prep(zs, zm, cos, sin, bf, g_kv[l].reshape(1, KV_RANK), w_kv,
                                             tp, tk_dsa, tk_fox, tk_sb)
        y_fox = _fox(zm, fcol, fvt, tq_att)
        y_sb = _sb(zm, svt, tq_att)
        y_dsa = _dsa(zs, zm, cos, sin, ikk, kk, vvt, 128, ktop)
        x2 = _merge(x2, gate, y_fox.reshape(B * S, WIDTH), y_dsa.reshape(B * S, WIDTH),
                    y_sb.reshape(B * S, WIDTH), zm2,
                    w_br_fox[l].astype(BF16), w_br_dsa[l].astype(BF16), w_br_sb[l].astype(BF16),
                    w_out[l].astype(BF16), g_final.reshape(1, D), S, min(512, S), l == depth - 1)
    return x2.reshape(B, S, D)
```

```python
import functools

import jax
import jax.numpy as jnp
from jax import lax
from jax.experimental import pallas as pl
from jax.experimental.pallas import tpu as pltpu

F32 = jnp.float32
BF16 = jnp.bfloat16
I32 = jnp.int32
HIGHEST = lax.Precision.HIGHEST

HEAD_DIM = 64
N_HEADS = 8
WIDTH = N_HEADS * HEAD_DIM
KV_RANK = 128
TOPK_MAX = 256
ROPE_THETA = 10000.0
EPS = 1e-6
LANES = 128
NEG = -1e30
INT_MIN = -(2 ** 31)
VMEM_LIMIT = 56 * 1024 * 1024

_MERGE, _FQ, _FK, _FV, _FG, _DQ, _DQR, _DG, _SQ, _SK, _SV, _SG = 0, 6, 7, 8, 9, 10, 11, 12, 13, 14, 15, 16
N_MAIN = 17 * WIDTH
_IQ, _IQR = 0, 1
_MISC, _CKV, _IK, _IKR = 8, 9, 10, 11
N_SMALL = 12 * LANES


def _params(*sem):
    return pltpu.CompilerParams(dimension_semantics=sem, vmem_limit_bytes=VMEM_LIMIT)


def _mod_kernel(c_ref, w_ref, b_ref, o_ref):
    c = c_ref[...]
    sc = c * jax.nn.sigmoid(c)
    o_ref[0] = jnp.dot(sc, w_ref[0], preferred_element_type=F32, precision=HIGHEST) + b_ref[0]


def _modulation(c, w_ada, b_ada):
    L, D, N = w_ada.shape
    B = c.shape[0]
    tn = 512
    return pl.pallas_call(
        _mod_kernel,
        grid=(L, N // tn),
        in_specs=[pl.BlockSpec((B, D), lambda l, n: (0, 0)),
                  pl.BlockSpec((1, D, tn), lambda l, n: (l, 0, n)),
                  pl.BlockSpec((1, 1, tn), lambda l, n: (l, 0, n))],
        out_specs=pl.BlockSpec((1, B, tn), lambda l, n: (l, 0, n)),
        out_shape=jax.ShapeDtypeStruct((L, B, N), F32),
        compiler_params=_params("arbitrary", "arbitrary"),
        name="adaln_mod",
    )(c, w_ada, b_ada.reshape(L, 1, N))


def _norm_proj_kernel(x_ref, g_ref, sc_ref, sh_ref, w_ref, o_ref, h_ref):
    @pl.when(pl.program_id(1) == 0)
    def _():
        x = x_ref[...]
        y = x * lax.rsqrt(jnp.mean(x * x, axis=-1, keepdims=True) + EPS)
        h = (y * g_ref[...]) * (1.0 + sc_ref[0]) + sh_ref[0]
        h_ref[...] = h.astype(BF16)

    o_ref[...] = jnp.dot(h_ref[...], w_ref[...], preferred_element_type=F32).astype(o_ref.dtype)


def _norm_proj(x2, g, scale, shift, w, S, out_dtype, tm, tn, name):
    M, D = x2.shape
    N = w.shape[1]
    per_b = S // tm
    return pl.pallas_call(
        _norm_proj_kernel,
        grid=(M // tm, N // tn),
        in_specs=[pl.BlockSpec((tm, D), lambda m, n: (m, 0)),
                  pl.BlockSpec((1, D), lambda m, n: (0, 0)),
                  pl.BlockSpec((1, 1, D), lambda m, n: (m // per_b, 0, 0)),
                  pl.BlockSpec((1, 1, D), lambda m, n: (m // per_b, 0, 0)),
                  pl.BlockSpec((D, tn), lambda m, n: (0, n))],
        out_specs=pl.BlockSpec((tm, tn), lambda m, n: (m, n)),
        out_shape=jax.ShapeDtypeStruct((M, N), out_dtype),
        scratch_shapes=[pltpu.VMEM((tm, D), BF16)],
        compiler_params=_params("arbitrary", "arbitrary"),
        name=name,
    )(x2, g, scale, shift, w)


def _store_transposed(x, out_ref, lead=()):
    x_t = x.astype(F32).T.astype(out_ref.dtype)
    chunk = out_ref.shape[-1]
    for i in range(x.shape[0] // chunk):
        out_ref[lead + (i,)] = x_t[:, i * chunk:(i + 1) * chunk]


def _prep_kernel(misc_ref, ckv_ref, ik_ref, ikr_ref, cos_ref, sin_ref, bf_ref, gkv_ref, wkv_ref, fv_ref, sv_ref,
                 fcol_ref, kk_ref, vvt_ref, ikk_ref, fvt_ref, svt_ref, carry_ref):
    @pl.when(pl.program_id(1) == 0)
    def _():
        carry_ref[...] = jnp.zeros_like(carry_ref)

    for p in range(N_HEADS // 2):
        _store_transposed(fv_ref[0, :, p * LANES:(p + 1) * LANES], fvt_ref, (0, p))
        _store_transposed(sv_ref[0, :, p * LANES:(p + 1) * LANES], svt_ref, (0, p))

    tp = misc_ref.shape[1]
    xf = misc_ref[0] + bf_ref[...]
    logf = jnp.minimum(xf, 0.0) - jnp.log1p(jnp.exp(-jnp.abs(xf)))
    r = lax.broadcasted_iota(I32, (tp, tp), 0)
    c = lax.broadcasted_iota(I32, (tp, tp), 1)
    tri = jnp.where(r >= c, 1.0, 0.0).astype(F32)
    cum = jnp.dot(tri, logf, preferred_element_type=F32, precision=HIGHEST) + carry_ref[...]
    carry_ref[...] = cum[tp - 1:tp, :]
    fcol_ref[0] = cum[:, :N_HEADS]

    cos = cos_ref[0]
    sin = sin_ref[0]
    ckv = ckv_ref[0]
    nrm = ckv * lax.rsqrt(jnp.mean(ckv * ckv, axis=-1, keepdims=True) + EPS) * gkv_ref[...]
    kv = jnp.dot(nrm.astype(BF16), wkv_ref[...], preferred_element_type=F32)
    kk_ref[0] = (kv[:, :LANES] * cos + kv[:, LANES:2 * LANES] * sin).astype(BF16)
    _store_transposed(kv[:, 2 * LANES:], vvt_ref, (0,))
    ikk_ref[0] = (ik_ref[0] * cos + ikr_ref[0] * sin).astype(BF16)


def _prep(zs, zm, cos, sin, bf, gkv, wkv, tp, tk_dsa, tk_fox, tk_sb):
    B, S, _ = zs.shape
    assert tp % tk_dsa == 0 and tp % tk_fox == 0 and tp % tk_sb == 0
    blk = lambda col: pl.BlockSpec((1, tp, LANES), lambda b, j: (b, j, col))
    row = pl.BlockSpec((1, tp, LANES), lambda b, j: (b, j, 0))
    wide = lambda col: pl.BlockSpec((1, tp, WIDTH), lambda b, j: (b, j, col))
    vt_spec = lambda tk: pl.BlockSpec((1, N_HEADS // 2, tp // tk, LANES, tk), lambda b, j: (b, 0, j, 0, 0))
    vt_shape = lambda tk: jax.ShapeDtypeStruct((B, N_HEADS // 2, S // tk, LANES, tk), BF16)
    return pl.pallas_call(
        _prep_kernel,
        grid=(B, S // tp),
        in_specs=[blk(_MISC), blk(_CKV), blk(_IK), blk(_IKR), row, row,
                  pl.BlockSpec((1, LANES), lambda b, j: (0, 0)),
                  pl.BlockSpec((1, LANES), lambda b, j: (0, 0)),
                  pl.BlockSpec((KV_RANK, 3 * LANES), lambda b, j: (0, 0)),
                  wide(_FV), wide(_SV)],
        out_specs=[pl.BlockSpec((1, tp, N_HEADS), lambda b, j: (b, j, 0)),
                   row, pl.BlockSpec((1, tp // tk_dsa, LANES, tk_dsa), lambda b, j: (b, j, 0, 0)), row,
                   vt_spec(tk_fox), vt_spec(tk_sb)],
        out_shape=[jax.ShapeDtypeStruct((B, S, N_HEADS), F32),
                   jax.ShapeDtypeStruct((B, S, LANES), BF16),
                   jax.ShapeDtypeStruct((B, S // tk_dsa, LANES, tk_dsa), BF16),
                   jax.ShapeDtypeStruct((B, S, LANES), BF16),
                   vt_shape(tk_fox), vt_shape(tk_sb)],
        scratch_shapes=[pltpu.VMEM((1, LANES), F32)],
        compiler_params=_params("arbitrary", "arbitrary"),
        name="prep",
    )(zs, zs, zs, zs, cos, sin, bf, gkv, wkv, zm, zm)


def _head_masks():
    lane = lax.broadcasted_iota(I32, (1, LANES), 1)
    lo = lane < HEAD_DIM
    return lo, jnp.logical_not(lo)


def _silu(g):
    return g * jax.nn.sigmoid(g)


def _pair_queries_t(q, masks):
    q = q.astype(F32)
    return jnp.concatenate([jnp.where(masks[hh], q, 0.0).T.astype(BF16) for hh in range(2)], axis=1)


def _pair_output(out0_t, out1_t, g):
    return jnp.concatenate([out0_t, out1_t], axis=0).T * _silu(g.astype(F32))


def _pipelined(n, produce, consume, carry):
    def pair(i, carry):
        produce(2 * i + 1, 1)
        carry = consume(2 * i, 0, carry, False)
        produce(2 * i + 2, 0)
        return consume(2 * i + 1, 1, carry, False)

    def odd_tail(carry):
        return consume(n - 1, 0, carry, True)

    def even_tail(carry):
        produce(n - 1, 1)
        carry = consume(n - 2, 0, carry, False)
        return consume(n - 1, 1, carry, True)

    produce(0, 0)
    carry = lax.fori_loop(0, (n - 1) // 2, pair, carry)
    return lax.cond(n % 2 == 1, odd_tail, even_tail, carry)


def _fox_kernel(q_ref, k_ref, vt_ref, g_ref, fcol_ref, o_ref, s_ref, *, t):
    qi = pl.program_id(2)
    qs_t = _pair_queries_t(q_ref[0], _head_masks())
    key_idx0 = lax.broadcasted_iota(I32, (t, t), 0)
    query_idx0 = lax.broadcasted_iota(I32, (t, t), 1)
    f0 = fcol_ref[0, 0, pl.ds(pl.multiple_of(qi * t, t), 1), :]

    def scores(j, slot):
        start = pl.multiple_of(j * t, t)
        b = f0 - fcol_ref[0, 0, pl.ds(start, t), :]
        bias = jnp.concatenate([jnp.broadcast_to(b[:, hh:hh + 1], (t, t)) for hh in range(2)], axis=1)
        s_ref[slot] = jnp.dot(k_ref[0, pl.ds(start, t), :], qs_t, preferred_element_type=F32) + bias

    def consume(j, slot, carry, diagonal):
        m, l, acc0, acc1 = carry
        s = s_ref[slot]
        if diagonal:
            valid = key_idx0 <= query_idx0
            s = jnp.where(jnp.concatenate([valid, valid], axis=1), s, NEG)
        m_new = jnp.maximum(m, jnp.max(s, axis=0, keepdims=True))
        alpha = jnp.exp(m - m_new)
        p = jnp.exp(s - m_new)
        l = alpha * l + jnp.sum(p, axis=0, keepdims=True)
        p = p.astype(BF16)
        v_t = vt_ref[0, 0, j]
        acc0 = alpha[:, :t] * acc0 + jnp.dot(v_t[:HEAD_DIM], p[:, :t], preferred_element_type=F32)
        acc1 = alpha[:, t:] * acc1 + jnp.dot(v_t[HEAD_DIM:], p[:, t:], preferred_element_type=F32)
        return m_new, l, acc0, acc1

    acc = jnp.zeros((HEAD_DIM, t), F32)
    init = (jnp.full((1, 2 * t), NEG, F32), jnp.zeros((1, 2 * t), F32), acc, acc)
    _, l, acc0, acc1 = _pipelined(qi + 1, scores, consume, init)
    o_ref[0] = _pair_output(acc0 / l[:, :t], acc1 / l[:, t:], g_ref[0]).astype(o_ref.dtype)


def _fox(zm, fcol, fvt, tq):
    B, S, _ = zm.shape
    tk = fvt.shape[4]
    assert tq == tk
    col = lambda base: (lambda b, h, i: (b, i, base * 4 + h))
    f4 = fcol.reshape(B, S, 4, 2).transpose(0, 2, 1, 3)
    return pl.pallas_call(
        functools.partial(_fox_kernel, t=tq),
        grid=(B, 4, S // tq),
        in_specs=[pl.BlockSpec((1, tq, LANES), col(_FQ)),
                  pl.BlockSpec((1, S, LANES), lambda b, h, i: (b, 0, _FK * 4 + h)),
                  pl.BlockSpec((1, 1, S // tk, LANES, tk), lambda b, h, i: (b, h, 0, 0, 0)),
                  pl.BlockSpec((1, tq, LANES), col(_FG)),
                  pl.BlockSpec((1, 1, S, 2), lambda b, h, i: (b, h, 0, 0))],
        out_specs=pl.BlockSpec((1, tq, LANES), lambda b, h, i: (b, i, h)),
        out_shape=jax.ShapeDtypeStruct((B, S, WIDTH), BF16),
        scratch_shapes=[pltpu.VMEM((2, tk, 2 * tq), F32)],
        compiler_params=_params("arbitrary", "arbitrary", "arbitrary"),
        name="fox_attention",
    )(zm, zm, fvt, zm, f4)


def _sb_kernel(q_ref, k_ref, vt_ref, g_ref, sfx_ref, o_ref, z_ref, *, tq, tk):
    assert tq == 2 * tk
    qi = pl.program_id(2)
    n_off = 2 * qi
    qs_t = _pair_queries_t(q_ref[0], _head_masks())
    key_idx0 = lax.broadcasted_iota(I32, (tk, tq), 0)
    query_idx0 = lax.broadcasted_iota(I32, (tk, tq), 1)

    def logits(u, slot):
        c = jnp.maximum(n_off + 1 - u, 0)
        z_ref[slot] = jnp.dot(k_ref[0, pl.ds(pl.multiple_of(c * tk, tk), tk), :], qs_t, preferred_element_type=F32)

    blk = sfx_ref.shape[0]

    def consume(u, slot, carry, masked):
        later, acc0, acc1 = carry
        c = n_off + 1 - u
        z = z_ref[slot]
        sp = jnp.maximum(z, 0.0) + jnp.log(1.0 + jnp.exp(-jnp.abs(z)))
        if masked:
            valid = (c - n_off) * tk + key_idx0 < query_idx0
            valid = jnp.concatenate([valid, valid], axis=1)
            sp = jnp.where(valid, sp, 0.0)
        hi = sp.astype(BF16)
        lo = (sp - hi.astype(F32)).astype(BF16)
        parts, above = [], None
        for b in reversed(range(tk // blk)):
            rows = slice(b * blk, (b + 1) * blk)
            inc = jnp.dot(sfx_ref[...], jnp.concatenate([hi[rows], lo[rows]], axis=0), preferred_element_type=F32)
            inc = inc if above is None else inc + above
            parts.insert(0, inc)
            above = inc[:1, :]
        incl = jnp.concatenate(parts, axis=0)
        a = jnp.exp(z - incl - later)
        if masked:
            a = jnp.where(valid, a, 0.0)
        a = a.astype(BF16)
        v_t = vt_ref[0, 0, c]
        acc0 = acc0 + jnp.dot(v_t[:HEAD_DIM], a[:, :tq], preferred_element_type=F32)
        acc1 = acc1 + jnp.dot(v_t[HEAD_DIM:], a[:, tq:], preferred_element_type=F32)
        return later + above, acc0, acc1

    def pair(i, carry, masked):
        logits(2 * i + 1, 1)
        carry = consume(2 * i, 0, carry, masked)
        logits(2 * i + 2, 0)
        return consume(2 * i + 1, 1, carry, masked)

    acc = jnp.zeros((HEAD_DIM, tq), F32)
    logits(0, 0)
    carry = pair(0, (jnp.zeros((1, 2 * tq), F32), acc, acc), True)
    _, acc0, acc1 = lax.fori_loop(1, qi + 1, functools.partial(pair, masked=False), carry)
    o_ref[0] = _pair_output(acc0, acc1, g_ref[0]).astype(o_ref.dtype)


def _sb(zm, svt, tq):
    B, S, _ = zm.shape
    tk = svt.shape[4]
    assert tq % tk == 0
    col = lambda base: (lambda b, h, i: (b, i, base * 4 + h))
    blk = LANES
    j = jnp.arange(2 * blk, dtype=I32)[None, :] % blk
    sfx = (j >= jnp.arange(blk, dtype=I32)[:, None]).astype(BF16)
    return pl.pallas_call(
        functools.partial(_sb_kernel, tq=tq, tk=tk),
        grid=(B, 4, S // tq),
        in_specs=[pl.BlockSpec((1, tq, LANES), col(_SQ)),
                  pl.BlockSpec((1, S, LANES), lambda b, h, i: (b, 0, _SK * 4 + h)),
                  pl.BlockSpec((1, 1, S // tk, LANES, tk), lambda b, h, i: (b, h, 0, 0, 0)),
                  pl.BlockSpec((1, tq, LANES), col(_SG)),
                  pl.BlockSpec((blk, 2 * blk), lambda b, h, i: (0, 0))],
        out_specs=pl.BlockSpec((1, tq, LANES), lambda b, h, i: (b, i, h)),
        out_shape=jax.ShapeDtypeStruct((B, S, WIDTH), BF16),
        scratch_shapes=[pltpu.VMEM((2, tk, 2 * tq), F32)],
        compiler_params=_params("arbitrary", "arbitrary", "arbitrary"),
        name="stick_breaking_attention",
    )(zm, zm, svt, zm, sfx)


def _dsa_kernel(iq_ref, iqr_ref, misc_ref, cos_ref, sin_ref, dq_ref, dqr_ref, dg_ref,
                ikk_ref, kk_ref, vvt_ref, o_ref, key_ref, bias_ref, cut_ref, s_ref, d_ref, *, tq, tk, ktop, seq_bits):
    qi = pl.program_id(1)
    q0 = qi * tq
    nch = (q0 + tq + tk - 1) // tk
    masks = _head_masks()
    cos = cos_ref[0]
    sin = sin_ref[0]
    cos4 = jnp.concatenate([cos] * 4, axis=1)
    sin4 = jnp.concatenate([sin] * 4, axis=1)
    iq = iq_ref[0] * cos4 + iqr_ref[0] * sin4
    qd = dq_ref[0].astype(F32) * cos4 + dqr_ref[0].astype(F32) * sin4

    def slab_t(x, h):
        blk = x[:, (h // 2) * LANES:(h // 2 + 1) * LANES]
        return jnp.where(masks[h % 2], blk, 0.0).T.astype(BF16)

    iq_t = [jnp.concatenate([slab_t(iq, 2 * p), slab_t(iq, 2 * p + 1)], axis=1) for p in range(N_HEADS // 2)]
    qs_t = jnp.concatenate([slab_t(qd, h) for h in range(N_HEADS)], axis=1)
    w_t = misc_ref[0].T[N_HEADS:2 * N_HEADS, :] * (N_HEADS ** -0.5)
    w_pair = [jnp.concatenate([w_t[2 * p:2 * p + 1], w_t[2 * p + 1:2 * p + 2]], axis=1) for p in range(N_HEADS // 2)]

    key_idx0 = lax.broadcasted_iota(I32, (tk, tq), 0)
    query_idx = q0 + lax.broadcasted_iota(I32, (tk, tq), 1)

    def index_dots(c, slot):
        ik = ikk_ref[0, pl.ds(pl.multiple_of(c * tk, tk), tk), :]
        for p in range(N_HEADS // 2):
            d_ref[slot, p] = jnp.dot(ik, iq_t[p], preferred_element_type=F32)

    def score_chunk(c, slot, carry, last):
        acc2 = jnp.zeros((tk, 2 * tq), F32)
        for p in range(N_HEADS // 2):
            acc2 = acc2 + jnp.maximum(d_ref[slot, p], 0.0) * w_pair[p]
        acc = acc2[:, :tq] + acc2[:, tq:]
        acc = jnp.where(acc == 0.0, 0.0, acc)
        bits = lax.bitcast_convert_type(acc, I32)
        key = bits ^ (lax.shift_right_arithmetic(bits, 31) & 0x7FFFFFFF)
        key = jnp.maximum(key, INT_MIN + 1)
        key_ref[c] = jnp.where(c * tk + key_idx0 <= query_idx, key, INT_MIN)
        return carry

    _pipelined(nch, index_dots, score_chunk, 0)

    n_acc = 4

    slab = lambda row: jnp.broadcast_to(row, (tk, tq))

    def count(pred):
        def body(c, cnt):
            hit = jnp.where(pred(key_ref[c], c * tk + key_idx0), 1.0, 0.0)
            return cnt + hit.reshape(n_acc, tk // (8 * n_acc), 8, tq).sum(axis=1)
        cnt = lax.fori_loop(0, nch, body, jnp.zeros((n_acc, 8, tq), F32))
        return jnp.sum(cnt.reshape(n_acc * 8, tq), axis=0, keepdims=True)

    def bit_body(b, prefix):
        cand_u = prefix | lax.shift_left(jnp.int32(1), 31 - b)
        cand = slab(cand_u ^ INT_MIN)
        n = count(lambda key, idx: key >= cand)
        return jnp.where(n >= ktop, cand_u, prefix)

    prefix = lax.fori_loop(0, 32, bit_body, jnp.zeros((1, tq), I32))
    thr_s = slab(prefix ^ INT_MIN)
    n_gt = count(lambda key, idx: key > thr_s)
    n_ge = count(lambda key, idx: key >= thr_s)
    need = ktop - n_gt

    cut_ref[...] = jnp.full(cut_ref.shape, 2 ** seq_bits, I32)

    @pl.when(jnp.max(n_ge) > ktop)
    def _():
        def idx_body(b, lo):
            cand_i = lo | lax.shift_left(jnp.int32(1), seq_bits - 1 - b)
            cand = slab(cand_i)
            g = count(lambda key, idx: (key == thr_s) & (idx < cand))
            return jnp.where(g < need, cand_i, lo)
        lo = lax.fori_loop(0, seq_bits, idx_body, jnp.zeros((1, tq), I32))
        cut_ref[...] = jnp.broadcast_to(lo, cut_ref.shape)

    thr = thr_s
    cut = slab(cut_ref[:1, :])

    def bias_chunk(c, carry):
        key = key_ref[c]
        sel = (key > thr) | ((key == thr) & (c * tk + key_idx0 <= cut))
        sel = sel & (key > INT_MIN)
        bias_ref[c] = jnp.where(sel, 0.0, NEG)
        return carry

    lax.fori_loop(0, nch, bias_chunk, 0)

    def logits(c, slot):
        k = kk_ref[0, pl.ds(pl.multiple_of(c * tk, tk), tk), :]
        s = jnp.dot(k, qs_t, preferred_element_type=F32)
        s_ref[slot] = s + jnp.concatenate([bias_ref[c]] * N_HEADS, axis=1)

    def consume(c, slot, carry, last):
        m, l, acc = carry
        s = s_ref[slot]
        m_new = jnp.maximum(m, jnp.max(s, axis=0, keepdims=True))
        alpha = jnp.exp(m - m_new)
        pr = jnp.exp(s - m_new)
        l = alpha * l + jnp.sum(pr, axis=0, keepdims=True)
        v_t = vvt_ref[0, c][:HEAD_DIM]
        acc = alpha * acc + jnp.dot(v_t, pr.astype(BF16), preferred_element_type=F32)
        return m_new, l, acc

    wide = N_HEADS * tq
    init = (jnp.full((1, wide), NEG, F32), jnp.zeros((1, wide), F32), jnp.zeros((HEAD_DIM, wide), F32))
    _, l, acc = _pipelined(nch, logits, consume, init)
    out_t = acc / l
    for p in range(N_HEADS // 2):
        g = dg_ref[0, :, p * LANES:(p + 1) * LANES]
        y = _pair_output(out_t[:, 2 * p * tq:(2 * p + 1) * tq], out_t[:, (2 * p + 1) * tq:(2 * p + 2) * tq], g)
        o_ref[0, :, p * LANES:(p + 1) * LANES] = y.astype(o_ref.dtype)


def _dsa(zs, zm, cos, sin, ikk, kk, vvt, tq, ktop):
    B, S, _ = zs.shape
    tk = vvt.shape[3]
    assert tq % LANES == 0
    wide = lambda col: pl.BlockSpec((1, tq, WIDTH), lambda b, i: (b, i, col))
    row = pl.BlockSpec((1, tq, LANES), lambda b, i: (b, i, 0))
    seq = pl.BlockSpec((1, S, LANES), lambda b, i: (b, 0, 0))
    seq_bits = max(1, (S - 1).bit_length())
    return pl.pallas_call(
        functools.partial(_dsa_kernel, tq=tq, tk=tk, ktop=ktop, seq_bits=seq_bits),
        grid=(B, S // tq),
        in_specs=[wide(_IQ), wide(_IQR),
                  pl.BlockSpec((1, tq, LANES), lambda b, i: (b, i, _MISC)),
                  row, row, wide(_DQ), wide(_DQR), wide(_DG), seq, seq,
                  pl.BlockSpec((1, S // tk, LANES, tk), lambda b, i: (b, 0, 0, 0))],
        out_specs=pl.BlockSpec((1, tq, WIDTH), lambda b, i: (b, i, 0)),
        out_shape=jax.ShapeDtypeStruct((B, S, WIDTH), BF16),
        scratch_shapes=[pltpu.VMEM((S // tk, tk, tq), I32),
                        pltpu.VMEM((S // tk, tk, tq), F32),
                        pltpu.VMEM((8, tq), I32),
                        pltpu.VMEM((2, tk, N_HEADS * tq), F32),
                        pltpu.VMEM((2, N_HEADS // 2, tk, 2 * tq), F32)],
        compiler_params=_params("arbitrary", "arbitrary"),
        name="dsa_attention",
    )(zs, zs, zs, cos, sin, zm, zm, zm, ikk, kk, vvt)


def _merge_kernel(x_ref, gate_ref, yf_ref, yd_ref, ys_ref, m0_ref, m1_ref, m2_ref,
                  wf_ref, wd_ref, ws_ref, wo_ref, gfin_ref, o_ref, *, final):
    def branch(y_ref, w_ref, m_ref):
        proj = jnp.dot(y_ref[...], w_ref[...], preferred_element_type=F32)
        return jax.nn.sigmoid(m_ref[...].astype(F32)) * proj

    mixed = branch(yf_ref, wf_ref, m0_ref) + branch(yd_ref, wd_ref, m1_ref) + branch(ys_ref, ws_ref, m2_ref)
    o = x_ref[...] + gate_ref[0] * jnp.dot(mixed.astype(BF16), wo_ref[...], preferred_element_type=F32)
    if final:
        o = o * lax.rsqrt(jnp.mean(o * o, axis=-1, keepdims=True) + EPS) * gfin_ref[...]
    o_ref[...] = o


def _merge(x2, gate, yf, yd, ys, zm2, wf, wd, ws, wo, gfin, S, tm, final):
    M, D = x2.shape
    per_b = S // tm
    ysp = pl.BlockSpec((tm, WIDTH), lambda m: (m, 0))
    msp = lambda j: pl.BlockSpec((tm, D), lambda m: (m, j))
    wsp = pl.BlockSpec((WIDTH, D), lambda m: (0, 0))
    return pl.pallas_call(
        functools.partial(_merge_kernel, final=final),
        grid=(M // tm,),
        in_specs=[pl.BlockSpec((tm, D), lambda m: (m, 0)),
                  pl.BlockSpec((1, 1, D), lambda m: (m // per_b, 0, 0)),
                  ysp, ysp, ysp, msp(0), msp(1), msp(2), wsp, wsp, wsp,
                  pl.BlockSpec((D, D), lambda m: (0, 0)),
                  pl.BlockSpec((1, D), lambda m: (0, 0))],
        out_specs=pl.BlockSpec((tm, D), lambda m: (m, 0)),
        out_shape=jax.ShapeDtypeStruct((M, D), F32),
        compiler_params=_params("arbitrary"),
        name="merge_out",
    )(x2, gate, yf, yd, ys, zm2, zm2, zm2, wf, wd, ws, wo, gfin)


def _rot_cols(w):
    d, n = w.shape
    w4 = w.reshape(d, n // HEAD_DIM, 2, HEAD_DIM // 2)
    return jnp.concatenate([-w4[:, :, 1:], w4[:, :, :1]], axis=2).reshape(d, n)


def _split_w_in(w):
    sizes = (WIDTH, WIDTH, WIDTH, N_HEADS, WIDTH,
             WIDTH, KV_RANK, WIDTH, HEAD_DIM, N_HEADS, WIDTH,
             WIDTH, WIDTH, WIDTH, WIDTH, w.shape[1] - (11 * WIDTH + 2 * N_HEADS + KV_RANK + HEAD_DIM))
    parts, off = [], 0
    for s in sizes:
        parts.append(w[:, off:off + s])
        off += s
    return parts


def _layer_weights(w_in, w_kv_up):
    (fq, fk, fv, ff, fg, dq, dckv, diq, dik, diw, dg, sq, sk, sv, sg, merge) = _split_w_in(w_in)
    scale = HEAD_DIM ** -0.5
    fq, dq, sq, diq = fq * scale, dq * scale, sq * scale, diq * scale
    w_main = jnp.concatenate([merge, fq, fk, fv, fg, dq, _rot_cols(dq), dg, sq, sk, sv, sg], axis=1)
    pad = jnp.zeros((w_in.shape[0], LANES - 2 * N_HEADS), w_in.dtype)
    dikr = _rot_cols(dik)
    w_small = jnp.concatenate([diq, _rot_cols(diq), ff, diw, pad, dckv, dik, dik, dikr, dikr], axis=1)
    wk, wv = w_kv_up[:, :HEAD_DIM], w_kv_up[:, HEAD_DIM:]
    wkr = _rot_cols(wk)
    w_kv = jnp.concatenate([wk, wk, wkr, wkr, wv, wv], axis=1)
    return w_main.astype(BF16), w_small.astype(BF16), w_kv.astype(BF16)


def kernel(x, c, positions, w_ada, b_ada, g_norm, w_in, b_fgt, g_kv, w_kv_up, w_br_fox, w_br_dsa, w_br_sb, w_out, g_final):
    B, S, D = x.shape
    depth = w_in.shape[0]
    assert D == 2 * WIDTH and S % 256 == 0
    ktop = min(TOPK_MAX, S // 4)
    tm = min(1024, S)
    tp = min(512, S)
    tq_att = min(512, S)
    tk_dsa = tk_fox = tp
    tk_sb = 256

    half = HEAD_DIM // 2
    inv_freq = ROPE_THETA ** (-jnp.arange(half, dtype=F32) / half)
    ang = positions.astype(F32)[..., None] * inv_freq
    cos = jnp.tile(jnp.cos(ang), (1, 1, LANES // half))
    sin = jnp.tile(jnp.sin(ang), (1, 1, LANES // half))

    mod = _modulation(c, w_ada, b_ada)
    x2 = x.reshape(B * S, D)
    for l in range(depth):
        shift = mod[l, :, None, :D]
        scale = mod[l, :, None, D:2 * D]
        gate = mod[l, :, None, 2 * D:]
        w_main, w_small, w_kv = _layer_weights(w_in[l], w_kv_up[l])
        g = g_norm[l].reshape(1, D)
        zm2 = _norm_proj(x2, g, scale, shift, w_main, S, BF16, tm, 512, "norm_proj_main")
        zs2 = _norm_proj(x2, g, scale, shift, w_small, S, F32, tm, 512, "norm_proj_small")
        zm = zm2.reshape(B, S, N_MAIN)
        zs = zs2.reshape(B, S, N_SMALL)
        bf = jnp.zeros((1, LANES), F32).at[0, :N_HEADS].set(b_fgt[l])
        fcol, kk, vvt, ikk, fvt, svt = _prep(zs, zm, cos, sin, bf, g_kv[l].reshape(1, KV_RANK), w_kv,
                                             tp, tk_dsa, tk_fox, tk_sb)
        y_fox = _fox(zm, fcol, fvt, tq_att)
        y_sb = _sb(zm, svt, tq_att)
        y_dsa = _dsa(zs, zm, cos, sin, ikk, kk, vvt, LANES, ktop)
        x2 = _merge(x2, gate, y_fox.reshape(B * S, WIDTH), y_dsa.reshape(B * S, WIDTH),
                    y_sb.reshape(B * S, WIDTH), zm2,
                    w_br_fox[l].astype(BF16), w_br_dsa[l].astype(BF16), w_br_sb[l].astype(BF16),
                    w_out[l].astype(BF16), g_final.reshape(1, D), S, min(512, S), l == depth - 1)
    return x2.reshape(B, S, D)
```

```python
import functools

import jax
import jax.numpy as jnp
from jax import lax
from jax.experimental import pallas as pl
from jax.experimental.pallas import tpu as pltpu

F32 = jnp.float32
BF16 = jnp.bfloat16
I32 = jnp.int32
I16 = jnp.int16
HIGHEST = lax.Precision.HIGHEST

HEAD_DIM = 64
N_HEADS = 8
WIDTH = N_HEADS * HEAD_DIM
KV_RANK = 128
TOPK_MAX = 256
ROPE_THETA = 10000.0
EPS = 1e-6
LANES = 128
NEG = -1e30
INT_MIN = -(2 ** 31)
VMEM_LIMIT = 56 * 1024 * 1024

_MERGE, _FQ, _FK, _FV, _FG, _DQ, _DQR, _DG, _SQ, _SK, _SV, _SG = 0, 6, 7, 8, 9, 10, 11, 12, 13, 14, 15, 16
N_MAIN = 17 * WIDTH
_IQ, _IQR = 0, 1
_MISC, _CKV, _IK, _IKR = 8, 9, 10, 11
N_SMALL = 12 * LANES


def _params(*sem):
    return pltpu.CompilerParams(dimension_semantics=sem, vmem_limit_bytes=VMEM_LIMIT)


def _mod_kernel(c_ref, w_ref, b_ref, o_ref):
    c = c_ref[...]
    sc = c * jax.nn.sigmoid(c)
    o_ref[0] = jnp.dot(sc, w_ref[0], preferred_element_type=F32, precision=HIGHEST) + b_ref[0]


def _modulation(c, w_ada, b_ada):
    L, D, N = w_ada.shape
    B = c.shape[0]
    tn = 512
    return pl.pallas_call(
        _mod_kernel,
        grid=(L, N // tn),
        in_specs=[pl.BlockSpec((B, D), lambda l, n: (0, 0)),
                  pl.BlockSpec((1, D, tn), lambda l, n: (l, 0, n)),
                  pl.BlockSpec((1, 1, tn), lambda l, n: (l, 0, n))],
        out_specs=pl.BlockSpec((1, B, tn), lambda l, n: (l, 0, n)),
        out_shape=jax.ShapeDtypeStruct((L, B, N), F32),
        compiler_params=_params("arbitrary", "arbitrary"),
        name="adaln_mod",
    )(c, w_ada, b_ada.reshape(L, 1, N))


def _norm_proj_kernel(x_ref, g_ref, sc_ref, sh_ref, w_ref, o_ref, h_ref):
    @pl.when(pl.program_id(1) == 0)
    def _():
        x = x_ref[...]
        y = x * lax.rsqrt(jnp.mean(x * x, axis=-1, keepdims=True) + EPS)
        h = (y * g_ref[...]) * (1.0 + sc_ref[0]) + sh_ref[0]
        h_ref[...] = h.astype(BF16)

    o_ref[...] = jnp.dot(h_ref[...], w_ref[...], preferred_element_type=F32).astype(o_ref.dtype)


def _norm_proj(x2, g, scale, shift, w, S, out_dtype, tm, tn, name):
    M, D = x2.shape
    N = w.shape[1]
    per_b = S // tm
    return pl.pallas_call(
        _norm_proj_kernel,
        grid=(M // tm, N // tn),
        in_specs=[pl.BlockSpec((tm, D), lambda m, n: (m, 0)),
                  pl.BlockSpec((1, D), lambda m, n: (0, 0)),
                  pl.BlockSpec((1, 1, D), lambda m, n: (m // per_b, 0, 0)),
                  pl.BlockSpec((1, 1, D), lambda m, n: (m // per_b, 0, 0)),
                  pl.BlockSpec((D, tn), lambda m, n: (0, n))],
        out_specs=pl.BlockSpec((tm, tn), lambda m, n: (m, n)),
        out_shape=jax.ShapeDtypeStruct((M, N), out_dtype),
        scratch_shapes=[pltpu.VMEM((tm, D), BF16)],
        compiler_params=_params("arbitrary", "arbitrary"),
        name=name,
    )(x2, g, scale, shift, w)


def _store_transposed(x, out_ref, lead=()):
    x_t = x.astype(F32).T.astype(out_ref.dtype)
    chunk = out_ref.shape[-1]
    for i in range(x.shape[0] // chunk):
        out_ref[lead + (i,)] = x_t[:, i * chunk:(i + 1) * chunk]


def _prep_kernel(misc_ref, ckv_ref, ik_ref, ikr_ref, cos_ref, sin_ref, bf_ref, gkv_ref, wkv_ref, fv_ref, sv_ref,
                 fcol_ref, kk_ref, vvt_ref, ikk_ref, fvt_ref, svt_ref, carry_ref):
    @pl.when(pl.program_id(1) == 0)
    def _():
        carry_ref[...] = jnp.zeros_like(carry_ref)

    for p in range(N_HEADS // 2):
        _store_transposed(fv_ref[0, :, p * LANES:(p + 1) * LANES], fvt_ref, (0, p))
        _store_transposed(sv_ref[0, :, p * LANES:(p + 1) * LANES], svt_ref, (0, p))

    tp = misc_ref.shape[1]
    xf = misc_ref[0] + bf_ref[...]
    logf = jnp.minimum(xf, 0.0) - jnp.log1p(jnp.exp(-jnp.abs(xf)))
    r = lax.broadcasted_iota(I32, (tp, tp), 0)
    c = lax.broadcasted_iota(I32, (tp, tp), 1)
    tri = jnp.where(r >= c, 1.0, 0.0).astype(F32)
    cum = jnp.dot(tri, logf, preferred_element_type=F32, precision=HIGHEST) + carry_ref[...]
    carry_ref[...] = cum[tp - 1:tp, :]
    fcol_ref[0] = cum[:, :N_HEADS]

    cos = cos_ref[0]
    sin = sin_ref[0]
    ckv = ckv_ref[0]
    nrm = ckv * lax.rsqrt(jnp.mean(ckv * ckv, axis=-1, keepdims=True) + EPS) * gkv_ref[...]
    kv = jnp.dot(nrm.astype(BF16), wkv_ref[...], preferred_element_type=F32)
    kk_ref[0] = (kv[:, :LANES] * cos + kv[:, LANES:2 * LANES] * sin).astype(BF16)
    _store_transposed(kv[:, 2 * LANES:], vvt_ref, (0,))
    ikk_ref[0] = (ik_ref[0] * cos + ikr_ref[0] * sin).astype(BF16)


def _prep(zs, zm, cos, sin, bf, gkv, wkv, tp, tk_dsa, tk_fox, tk_sb):
    B, S, _ = zs.shape
    assert tp % tk_dsa == 0 and tp % tk_fox == 0 and tp % tk_sb == 0
    blk = lambda col: pl.BlockSpec((1, tp, LANES), lambda b, j: (b, j, col))
    row = pl.BlockSpec((1, tp, LANES), lambda b, j: (b, j, 0))
    wide = lambda col: pl.BlockSpec((1, tp, WIDTH), lambda b, j: (b, j, col))
    vt_spec = lambda tk: pl.BlockSpec((1, N_HEADS // 2, tp // tk, LANES, tk), lambda b, j: (b, 0, j, 0, 0))
    vt_shape = lambda tk: jax.ShapeDtypeStruct((B, N_HEADS // 2, S // tk, LANES, tk), BF16)
    return pl.pallas_call(
        _prep_kernel,
        grid=(B, S // tp),
        in_specs=[blk(_MISC), blk(_CKV), blk(_IK), blk(_IKR), row, row,
                  pl.BlockSpec((1, LANES), lambda b, j: (0, 0)),
                  pl.BlockSpec((1, LANES), lambda b, j: (0, 0)),
                  pl.BlockSpec((KV_RANK, 3 * LANES), lambda b, j: (0, 0)),
                  wide(_FV), wide(_SV)],
        out_specs=[pl.BlockSpec((1, tp, N_HEADS), lambda b, j: (b, j, 0)),
                   row, pl.BlockSpec((1, tp // tk_dsa, LANES, tk_dsa), lambda b, j: (b, j, 0, 0)), row,
                   vt_spec(tk_fox), vt_spec(tk_sb)],
        out_shape=[jax.ShapeDtypeStruct((B, S, N_HEADS), F32),
                   jax.ShapeDtypeStruct((B, S, LANES), BF16),
                   jax.ShapeDtypeStruct((B, S // tk_dsa, LANES, tk_dsa), BF16),
                   jax.ShapeDtypeStruct((B, S, LANES), BF16),
                   vt_shape(tk_fox), vt_shape(tk_sb)],
        scratch_shapes=[pltpu.VMEM((1, LANES), F32)],
        compiler_params=_params("arbitrary", "arbitrary"),
        name="prep",
    )(zs, zs, zs, zs, cos, sin, bf, gkv, wkv, zm, zm)


def _head_masks():
    lane = lax.broadcasted_iota(I32, (1, LANES), 1)
    lo = lane < HEAD_DIM
    return lo, jnp.logical_not(lo)


def _silu(g):
    return g * jax.nn.sigmoid(g)


def _pair_queries_t(q, masks):
    q = q.astype(F32)
    return jnp.concatenate([jnp.where(masks[hh], q, 0.0).T.astype(BF16) for hh in range(2)], axis=1)


def _pair_output(out0_t, out1_t, g):
    return jnp.concatenate([out0_t, out1_t], axis=0).T * _silu(g.astype(F32))


def _pipelined(n, produce, consume, carry):
    def pair(i, carry):
        produce(2 * i + 1, 1)
        carry = consume(2 * i, 0, carry, False)
        produce(2 * i + 2, 0)
        return consume(2 * i + 1, 1, carry, False)

    def odd_tail(carry):
        return consume(n - 1, 0, carry, True)

    def even_tail(carry):
        produce(n - 1, 1)
        carry = consume(n - 2, 0, carry, False)
        return consume(n - 1, 1, carry, True)

    produce(0, 0)
    carry = lax.fori_loop(0, (n - 1) // 2, pair, carry)
    return lax.cond(n % 2 == 1, odd_tail, even_tail, carry)


def _fox_kernel(q_ref, k_ref, vt_ref, g_ref, fcol_ref, o_ref, s_ref, *, t):
    qi = pl.program_id(2)
    qs_t = _pair_queries_t(q_ref[0], _head_masks())
    key_idx0 = lax.broadcasted_iota(I32, (t, t), 0)
    query_idx0 = lax.broadcasted_iota(I32, (t, t), 1)
    f0 = fcol_ref[0, 0, pl.ds(pl.multiple_of(qi * t, t), 1), :]

    def scores(j, slot):
        start = pl.multiple_of(j * t, t)
        b = f0 - fcol_ref[0, 0, pl.ds(start, t), :]
        bias = jnp.concatenate([jnp.broadcast_to(b[:, hh:hh + 1], (t, t)) for hh in range(2)], axis=1)
        s_ref[slot] = jnp.dot(k_ref[0, pl.ds(start, t), :], qs_t, preferred_element_type=F32) + bias

    def consume(j, slot, carry, diagonal):
        m, l, acc0, acc1 = carry
        s = s_ref[slot]
        if diagonal:
            valid = key_idx0 <= query_idx0
            s = jnp.where(jnp.concatenate([valid, valid], axis=1), s, NEG)
        m_new = jnp.maximum(m, jnp.max(s, axis=0, keepdims=True))
        alpha = jnp.exp(m - m_new)
        p = jnp.exp(s - m_new)
        l = alpha * l + jnp.sum(p, axis=0, keepdims=True)
        p = p.astype(BF16)
        v_t = vt_ref[0, 0, j]
        acc0 = alpha[:, :t] * acc0 + jnp.dot(v_t[:HEAD_DIM], p[:, :t], preferred_element_type=F32)
        acc1 = alpha[:, t:] * acc1 + jnp.dot(v_t[HEAD_DIM:], p[:, t:], preferred_element_type=F32)
        return m_new, l, acc0, acc1

    acc = jnp.zeros((HEAD_DIM, t), F32)
    init = (jnp.full((1, 2 * t), NEG, F32), jnp.zeros((1, 2 * t), F32), acc, acc)
    _, l, acc0, acc1 = _pipelined(qi + 1, scores, consume, init)
    o_ref[0] = _pair_output(acc0 / l[:, :t], acc1 / l[:, t:], g_ref[0]).astype(o_ref.dtype)


def _fox(zm, fcol, fvt, tq):
    B, S, _ = zm.shape
    tk = fvt.shape[4]
    assert tq == tk
    col = lambda base: (lambda b, h, i: (b, i, base * 4 + h))
    f4 = fcol.reshape(B, S, 4, 2).transpose(0, 2, 1, 3)
    return pl.pallas_call(
        functools.partial(_fox_kernel, t=tq),
        grid=(B, 4, S // tq),
        in_specs=[pl.BlockSpec((1, tq, LANES), col(_FQ)),
                  pl.BlockSpec((1, S, LANES), lambda b, h, i: (b, 0, _FK * 4 + h)),
                  pl.BlockSpec((1, 1, S // tk, LANES, tk), lambda b, h, i: (b, h, 0, 0, 0)),
                  pl.BlockSpec((1, tq, LANES), col(_FG)),
                  pl.BlockSpec((1, 1, S, 2), lambda b, h, i: (b, h, 0, 0))],
        out_specs=pl.BlockSpec((1, tq, LANES), lambda b, h, i: (b, i, h)),
        out_shape=jax.ShapeDtypeStruct((B, S, WIDTH), BF16),
        scratch_shapes=[pltpu.VMEM((2, tk, 2 * tq), F32)],
        compiler_params=_params("arbitrary", "arbitrary", "arbitrary"),
        name="fox_attention",
    )(zm, zm, fvt, zm, f4)


def _sb_kernel(q_ref, k_ref, vt_ref, g_ref, sfx_ref, o_ref, z_ref, *, tq, tk):
    assert tq == 2 * tk
    qi = pl.program_id(2)
    n_off = 2 * qi
    qs_t = _pair_queries_t(q_ref[0], _head_masks())
    key_idx0 = lax.broadcasted_iota(I32, (tk, tq), 0)
    query_idx0 = lax.broadcasted_iota(I32, (tk, tq), 1)

    def logits(u, slot):
        c = jnp.maximum(n_off + 1 - u, 0)
        z_ref[slot] = jnp.dot(k_ref[0, pl.ds(pl.multiple_of(c * tk, tk), tk), :], qs_t, preferred_element_type=F32)

    blk = sfx_ref.shape[0]

    def consume(u, slot, carry, masked):
        later, acc0, acc1 = carry
        c = n_off + 1 - u
        z = z_ref[slot]
        sp = jnp.maximum(z, 0.0) + jnp.log(1.0 + jnp.exp(-jnp.abs(z)))
        if masked:
            valid = (c - n_off) * tk + key_idx0 < query_idx0
            valid = jnp.concatenate([valid, valid], axis=1)
            sp = jnp.where(valid, sp, 0.0)
        hi = sp.astype(BF16)
        lo = (sp - hi.astype(F32)).astype(BF16)
        parts, above = [], None
        for b in reversed(range(tk // blk)):
            rows = slice(b * blk, (b + 1) * blk)
            inc = jnp.dot(sfx_ref[...], jnp.concatenate([hi[rows], lo[rows]], axis=0), preferred_element_type=F32)
            inc = inc if above is None else inc + above
            parts.insert(0, inc)
            above = inc[:1, :]
        incl = jnp.concatenate(parts, axis=0)
        a = jnp.exp(z - incl - later)
        if masked:
            a = jnp.where(valid, a, 0.0)
        a = a.astype(BF16)
        v_t = vt_ref[0, 0, c]
        acc0 = acc0 + jnp.dot(v_t[:HEAD_DIM], a[:, :tq], preferred_element_type=F32)
        acc1 = acc1 + jnp.dot(v_t[HEAD_DIM:], a[:, tq:], preferred_element_type=F32)
        return later + above, acc0, acc1

    def pair(i, carry, masked):
        logits(2 * i + 1, 1)
        carry = consume(2 * i, 0, carry, masked)
        logits(2 * i + 2, 0)
        return consume(2 * i + 1, 1, carry, masked)

    acc = jnp.zeros((HEAD_DIM, tq), F32)
    logits(0, 0)
    carry = pair(0, (jnp.zeros((1, 2 * tq), F32), acc, acc), True)
    _, acc0, acc1 = lax.fori_loop(1, qi + 1, functools.partial(pair, masked=False), carry)
    o_ref[0] = _pair_output(acc0, acc1, g_ref[0]).astype(o_ref.dtype)


def _sb(zm, svt, tq):
    B, S, _ = zm.shape
    tk = svt.shape[4]
    assert tq % tk == 0
    col = lambda base: (lambda b, h, i: (b, i, base * 4 + h))
    blk = LANES
    j = jnp.arange(2 * blk, dtype=I32)[None, :] % blk
    sfx = (j >= jnp.arange(blk, dtype=I32)[:, None]).astype(BF16)
    return pl.pallas_call(
        functools.partial(_sb_kernel, tq=tq, tk=tk),
        grid=(B, 4, S // tq),
        in_specs=[pl.BlockSpec((1, tq, LANES), col(_SQ)),
                  pl.BlockSpec((1, S, LANES), lambda b, h, i: (b, 0, _SK * 4 + h)),
                  pl.BlockSpec((1, 1, S // tk, LANES, tk), lambda b, h, i: (b, h, 0, 0, 0)),
                  pl.BlockSpec((1, tq, LANES), col(_SG)),
                  pl.BlockSpec((blk, 2 * blk), lambda b, h, i: (0, 0))],
        out_specs=pl.BlockSpec((1, tq, LANES), lambda b, h, i: (b, i, h)),
        out_shape=jax.ShapeDtypeStruct((B, S, WIDTH), BF16),
        scratch_shapes=[pltpu.VMEM((2, tk, 2 * tq), F32)],
        compiler_params=_params("arbitrary", "arbitrary", "arbitrary"),
        name="stick_breaking_attention",
    )(zm, zm, svt, zm, sfx)


def _dsa_kernel(iq_ref, iqr_ref, misc_ref, cos_ref, sin_ref, dq_ref, dqr_ref, dg_ref,
                ikk_ref, kk_ref, vvt_ref, o_ref, key_ref, bias_ref, cut_ref, s_ref, d_ref, half_ref, *, tq, tk, ktop, seq_bits):
    qi = pl.program_id(1)
    q0 = qi * tq
    nch = (q0 + tq + tk - 1) // tk
    masks = _head_masks()
    cos = cos_ref[0]
    sin = sin_ref[0]
    cos4 = jnp.concatenate([cos] * 4, axis=1)
    sin4 = jnp.concatenate([sin] * 4, axis=1)
    iq = iq_ref[0] * cos4 + iqr_ref[0] * sin4
    qd = dq_ref[0].astype(F32) * cos4 + dqr_ref[0].astype(F32) * sin4

    def slab_t(x, h):
        blk = x[:, (h // 2) * LANES:(h // 2 + 1) * LANES]
        return jnp.where(masks[h % 2], blk, 0.0).T.astype(BF16)

    iq_t = [jnp.concatenate([slab_t(iq, 2 * p), slab_t(iq, 2 * p + 1)], axis=1) for p in range(N_HEADS // 2)]
    qs_t = jnp.concatenate([slab_t(qd, h) for h in range(N_HEADS)], axis=1)
    w_t = misc_ref[0].T[N_HEADS:2 * N_HEADS, :] * (N_HEADS ** -0.5)
    w_pair = [jnp.concatenate([w_t[2 * p:2 * p + 1], w_t[2 * p + 1:2 * p + 2]], axis=1) for p in range(N_HEADS // 2)]

    key_idx0 = lax.broadcasted_iota(I32, (tk, tq), 0)
    query_idx = q0 + lax.broadcasted_iota(I32, (tk, tq), 1)

    def index_dots(c, slot):
        ik = ikk_ref[0, pl.ds(pl.multiple_of(c * tk, tk), tk), :]
        for p in range(N_HEADS // 2):
            d_ref[slot, p] = jnp.dot(ik, iq_t[p], preferred_element_type=F32)

    def score_chunk(c, slot, carry, last):
        acc2 = jnp.zeros((tk, 2 * tq), F32)
        for p in range(N_HEADS // 2):
            acc2 = acc2 + jnp.maximum(d_ref[slot, p], 0.0) * w_pair[p]
        acc = acc2[:, :tq] + acc2[:, tq:]
        acc = jnp.where(acc == 0.0, 0.0, acc)
        bits = lax.bitcast_convert_type(acc, I32)
        key = bits ^ (lax.shift_right_arithmetic(bits, 31) & 0x7FFFFFFF)
        key = jnp.maximum(key, INT_MIN + 1)
        key = jnp.where(c * tk + key_idx0 <= query_idx, key, INT_MIN)
        key_ref[c] = key
        half_ref[c] = lax.shift_right_arithmetic(key, 16).astype(I16)
        return carry

    _pipelined(nch, index_dots, score_chunk, 0)

    n_acc = 4

    slab = lambda row: jnp.broadcast_to(row, (tk, tq))

    def count16(cand):
        cand = jnp.broadcast_to(cand, (tk, tq))
        one, zero = jnp.ones((), I16), jnp.zeros((), I16)

        def body(c, cnt):
            hit = jnp.where(half_ref[c] >= cand, one, zero)
            hit = hit.reshape(n_acc, tk // (16 * n_acc), 16, tq)
            for g in range(hit.shape[1]):
                cnt = cnt + hit[:, g]
            return cnt
        cnt = lax.fori_loop(0, nch, body, jnp.zeros((n_acc, 16, tq), I16))
        return jnp.sum(cnt.astype(I32).reshape(n_acc * 16, tq), axis=0, keepdims=True)

    def select16(rank):
        def bit_body(b, prefix):
            cand_u = prefix | lax.shift_left(jnp.int32(1), 15 - b)
            n = count16((cand_u - 32768).astype(I16))
            return jnp.where(n >= rank, cand_u, prefix)
        return lax.fori_loop(0, 16, bit_body, jnp.zeros((1, tq), I32))

    def count(pred):
        def body(c, cnt):
            hit = jnp.where(pred(key_ref[c], c * tk + key_idx0), 1.0, 0.0)
            return cnt + hit.reshape(n_acc, tk // (8 * n_acc), 8, tq).sum(axis=1)
        cnt = lax.fori_loop(0, nch, body, jnp.zeros((n_acc, 8, tq), F32))
        return jnp.sum(cnt.reshape(n_acc * 8, tq), axis=0, keepdims=True)

    hi_u = select16(jnp.full((1, tq), ktop, I32))
    n_next = count16((jnp.minimum(hi_u + 1, 65535) - 32768).astype(I16))
    n_above = jnp.where(hi_u == 65535, 0, n_next)
    hi_s = hi_u - 32768
    hi_slab = jnp.broadcast_to(hi_s.astype(I16), (tk, tq))

    def low_chunk(c, carry):
        low = ((key_ref[c] & 0xFFFF) - 32768).astype(I16)
        half_ref[c] = jnp.where(half_ref[c] == hi_slab, low, jnp.full((), -32768, I16))
        return carry

    lax.fori_loop(0, nch, low_chunk, 0)
    lo_u = select16(ktop - n_above)
    thr_s = slab(lax.shift_left(hi_s, 16) | lo_u)
    n_gt = count(lambda key, idx: key > thr_s)
    n_ge = count(lambda key, idx: key >= thr_s)
    need = ktop - n_gt

    cut_ref[...] = jnp.full(cut_ref.shape, 2 ** seq_bits, I32)

    @pl.when(jnp.max(n_ge) > ktop)
    def _():
        def idx_body(b, lo):
            cand_i = lo | lax.shift_left(jnp.int32(1), seq_bits - 1 - b)
            cand = slab(cand_i)
            g = count(lambda key, idx: (key == thr_s) & (idx < cand))
            return jnp.where(g < need, cand_i, lo)
        lo = lax.fori_loop(0, seq_bits, idx_body, jnp.zeros((1, tq), I32))
        cut_ref[...] = jnp.broadcast_to(lo, cut_ref.shape)

    thr = thr_s
    cut = slab(cut_ref[:1, :])

    def bias_chunk(c, carry):
        key = key_ref[c]
        sel = (key > thr) | ((key == thr) & (c * tk + key_idx0 <= cut))
        sel = sel & (key > INT_MIN)
        bias_ref[c] = jnp.where(sel, 0.0, NEG)
        return carry

    lax.fori_loop(0, nch, bias_chunk, 0)

    def logits(c, slot):
        k = kk_ref[0, pl.ds(pl.multiple_of(c * tk, tk), tk), :]
        s = jnp.dot(k, qs_t, preferred_element_type=F32)
        s_ref[slot] = s + jnp.concatenate([bias_ref[c]] * N_HEADS, axis=1)

    def consume(c, slot, carry, last):
        m, l, acc = carry
        s = s_ref[slot]
        m_new = jnp.maximum(m, jnp.max(s, axis=0, keepdims=True))
        alpha = jnp.exp(m - m_new)
        pr = jnp.exp(s - m_new)
        l = alpha * l + jnp.sum(pr, axis=0, keepdims=True)
        v_t = vvt_ref[0, c][:HEAD_DIM]
        acc = alpha * acc + jnp.dot(v_t, pr.astype(BF16), preferred_element_type=F32)
        return m_new, l, acc

    wide = N_HEADS * tq
    init = (jnp.full((1, wide), NEG, F32), jnp.zeros((1, wide), F32), jnp.zeros((HEAD_DIM, wide), F32))
    _, l, acc = _pipelined(nch, logits, consume, init)
    out_t = acc / l
    for p in range(N_HEADS // 2):
        g = dg_ref[0, :, p * LANES:(p + 1) * LANES]
        y = _pair_output(out_t[:, 2 * p * tq:(2 * p + 1) * tq], out_t[:, (2 * p + 1) * tq:(2 * p + 2) * tq], g)
        o_ref[0, :, p * LANES:(p + 1) * LANES] = y.astype(o_ref.dtype)


def _dsa(zs, zm, cos, sin, ikk, kk, vvt, tq, ktop):
    B, S, _ = zs.shape
    tk = vvt.shape[3]
    assert tq % LANES == 0
    wide = lambda col: pl.BlockSpec((1, tq, WIDTH), lambda b, i: (b, i, col))
    row = pl.BlockSpec((1, tq, LANES), lambda b, i: (b, i, 0))
    seq = pl.BlockSpec((1, S, LANES), lambda b, i: (b, 0, 0))
    seq_bits = max(1, (S - 1).bit_length())
    return pl.pallas_call(
        functools.partial(_dsa_kernel, tq=tq, tk=tk, ktop=ktop, seq_bits=seq_bits),
        grid=(B, S // tq),
        in_specs=[wide(_IQ), wide(_IQR),
                  pl.BlockSpec((1, tq, LANES), lambda b, i: (b, i, _MISC)),
                  row, row, wide(_DQ), wide(_DQR), wide(_DG), seq, seq,
                  pl.BlockSpec((1, S // tk, LANES, tk), lambda b, i: (b, 0, 0, 0))],
        out_specs=pl.BlockSpec((1, tq, WIDTH), lambda b, i: (b, i, 0)),
        out_shape=jax.ShapeDtypeStruct((B, S, WIDTH), BF16),
        scratch_shapes=[pltpu.VMEM((S // tk, tk, tq), I32),
                        pltpu.VMEM((S // tk, tk, tq), F32),
                        pltpu.VMEM((8, tq), I32),
                        pltpu.VMEM((2, tk, N_HEADS * tq), F32),
                        pltpu.VMEM((2, N_HEADS // 2, tk, 2 * tq), F32),
                        pltpu.VMEM((S // tk, tk, tq), I16)],
        compiler_params=_params("arbitrary", "arbitrary"),
        name="dsa_attention",
    )(zs, zs, zs, cos, sin, zm, zm, zm, ikk, kk, vvt)


def _merge_kernel(x_ref, gate_ref, yf_ref, yd_ref, ys_ref, m0_ref, m1_ref, m2_ref,
                  wf_ref, wd_ref, ws_ref, wo_ref, gfin_ref, o_ref, *, final):
    def branch(y_ref, w_ref, m_ref):
        proj = jnp.dot(y_ref[...], w_ref[...], preferred_element_type=F32)
        return jax.nn.sigmoid(m_ref[...].astype(F32)) * proj

    mixed = branch(yf_ref, wf_ref, m0_ref) + branch(yd_ref, wd_ref, m1_ref) + branch(ys_ref, ws_ref, m2_ref)
    o = x_ref[...] + gate_ref[0] * jnp.dot(mixed.astype(BF16), wo_ref[...], preferred_element_type=F32)
    if final:
        o = o * lax.rsqrt(jnp.mean(o * o, axis=-1, keepdims=True) + EPS) * gfin_ref[...]
    o_ref[...] = o


def _merge(x2, gate, yf, yd, ys, zm2, wf, wd, ws, wo, gfin, S, tm, final):
    M, D = x2.shape
    per_b = S // tm
    ysp = pl.BlockSpec((tm, WIDTH), lambda m: (m, 0))
    msp = lambda j: pl.BlockSpec((tm, D), lambda m: (m, j))
    wsp = pl.BlockSpec((WIDTH, D), lambda m: (0, 0))
    return pl.pallas_call(
        functools.partial(_merge_kernel, final=final),
        grid=(M // tm,),
        in_specs=[pl.BlockSpec((tm, D), lambda m: (m, 0)),
                  pl.BlockSpec((1, 1, D), lambda m: (m // per_b, 0, 0)),
                  ysp, ysp, ysp, msp(0), msp(1), msp(2), wsp, wsp, wsp,
                  pl.BlockSpec((D, D), lambda m: (0, 0)),
                  pl.BlockSpec((1, D), lambda m: (0, 0))],
        out_specs=pl.BlockSpec((tm, D), lambda m: (m, 0)),
        out_shape=jax.ShapeDtypeStruct((M, D), F32),
        compiler_params=_params("arbitrary"),
        name="merge_out",
    )(x2, gate, yf, yd, ys, zm2, zm2, zm2, wf, wd, ws, wo, gfin)


def _rot_cols(w):
    d, n = w.shape
    w4 = w.reshape(d, n // HEAD_DIM, 2, HEAD_DIM // 2)
    return jnp.concatenate([-w4[:, :, 1:], w4[:, :, :1]], axis=2).reshape(d, n)


def _split_w_in(w):
    sizes = (WIDTH, WIDTH, WIDTH, N_HEADS, WIDTH,
             WIDTH, KV_RANK, WIDTH, HEAD_DIM, N_HEADS, WIDTH,
             WIDTH, WIDTH, WIDTH, WIDTH, w.shape[1] - (11 * WIDTH + 2 * N_HEADS + KV_RANK + HEAD_DIM))
    parts, off = [], 0
    for s in sizes:
        parts.append(w[:, off:off + s])
        off += s
    return parts


def _layer_weights(w_in, w_kv_up):
    (fq, fk, fv, ff, fg, dq, dckv, diq, dik, diw, dg, sq, sk, sv, sg, merge) = _split_w_in(w_in)
    scale = HEAD_DIM ** -0.5
    fq, dq, sq, diq = fq * scale, dq * scale, sq * scale, diq * scale
    w_main = jnp.concatenate([merge, fq, fk, fv, fg, dq, _rot_cols(dq), dg, sq, sk, sv, sg], axis=1)
    pad = jnp.zeros((w_in.shape[0], LANES - 2 * N_HEADS), w_in.dtype)
    dikr = _rot_cols(dik)
    w_small = jnp.concatenate([diq, _rot_cols(diq), ff, diw, pad, dckv, dik, dik, dikr, dikr], axis=1)
    wk, wv = w_kv_up[:, :HEAD_DIM], w_kv_up[:, HEAD_DIM:]
    wkr = _rot_cols(wk)
    w_kv = jnp.concatenate([wk, wk, wkr, wkr, wv, wv], axis=1)
    return w_main.astype(BF16), w_small.astype(BF16), w_kv.astype(BF16)


def kernel(x, c, positions, w_ada, b_ada, g_norm, w_in, b_fgt, g_kv, w_kv_up, w_br_fox, w_br_dsa, w_br_sb, w_out, g_final):
    B, S, D = x.shape
    depth = w_in.shape[0]
    assert D == 2 * WIDTH and S % 256 == 0
    ktop = min(TOPK_MAX, S // 4)
    tm = min(1024, S)
    tp = min(512, S)
    tq_att = min(512, S)
    tk_dsa = tk_fox = tp
    tk_sb = 256

    half = HEAD_DIM // 2
    inv_freq = ROPE_THETA ** (-jnp.arange(half, dtype=F32) / half)
    ang = positions.astype(F32)[..., None] * inv_freq
    cos = jnp.tile(jnp.cos(ang), (1, 1, LANES // half))
    sin = jnp.tile(jnp.sin(ang), (1, 1, LANES // half))

    mod = _modulation(c, w_ada, b_ada)
    x2 = x.reshape(B * S, D)
    for l in range(depth):
        shift = mod[l, :, None, :D]
        scale = mod[l, :, None, D:2 * D]
        gate = mod[l, :, None, 2 * D:]
        w_main, w_small, w_kv = _layer_weights(w_in[l], w_kv_up[l])
        g = g_norm[l].reshape(1, D)
        zm2 = _norm_proj(x2, g, scale, shift, w_main, S, BF16, tm, 512, "norm_proj_main")
        zs2 = _norm_proj(x2, g, scale, shift, w_small, S, F32, tm, 512, "norm_proj_small")
        zm = zm2.reshape(B, S, N_MAIN)
        zs = zs2.reshape(B, S, N_SMALL)
        bf = jnp.zeros((1, LANES), F32).at[0, :N_HEADS].set(b_fgt[l])
        fcol, kk, vvt, ikk, fvt, svt = _prep(zs, zm, cos, sin, bf, g_kv[l].reshape(1, KV_RANK), w_kv,
                                             tp, tk_dsa, tk_fox, tk_sb)
        y_fox = _fox(zm, fcol, fvt, tq_att)
        y_sb = _sb(zm, svt, tq_att)
        y_dsa = _dsa(zs, zm, cos, sin, ikk, kk, vvt, LANES, ktop)
        x2 = _merge(x2, gate, y_fox.reshape(B * S, WIDTH), y_dsa.reshape(B * S, WIDTH),
                    y_sb.reshape(B * S, WIDTH), zm2,
                    w_br_fox[l].astype(BF16), w_br_dsa[l].astype(BF16), w_br_sb[l].astype(BF16),
                    w_out[l].astype(BF16), g_final.reshape(1, D), S, min(512, S), l == depth - 1)
    return x2.reshape(B, S, D)
```

```python
import functools

import jax
import jax.numpy as jnp
from jax import lax
from jax.experimental import pallas as pl
from jax.experimental.pallas import tpu as pltpu

F32 = jnp.float32
BF16 = jnp.bfloat16
I32 = jnp.int32
HIGHEST = lax.Precision.HIGHEST

HEAD_DIM = 64
N_HEADS = 8
WIDTH = N_HEADS * HEAD_DIM
KV_RANK = 128
TOPK_MAX = 256
ROPE_THETA = 10000.0
EPS = 1e-6
LANES = 128
NEG = -1e30
INT_MIN = -(2 ** 31)
LOG2E = 1.4426950408889634
VMEM_LIMIT = 56 * 1024 * 1024

_MERGE, _FQ, _FK, _FV, _FG, _DQ, _DQR, _DG, _SQ, _SK, _SV, _SG = 0, 6, 7, 8, 9, 10, 11, 12, 13, 14, 15, 16
N_MAIN = 17 * WIDTH
_IQ, _IQR = 0, 1
_MISC, _CKV, _IK, _IKR = 8, 9, 10, 11
N_SMALL = 12 * LANES


def _params(*sem):
    return pltpu.CompilerParams(dimension_semantics=sem, vmem_limit_bytes=VMEM_LIMIT)


def _mod_kernel(c_ref, w_ref, b_ref, o_ref):
    c = c_ref[...]
    sc = c * jax.nn.sigmoid(c)
    o_ref[0] = jnp.dot(sc, w_ref[0], preferred_element_type=F32, precision=HIGHEST) + b_ref[0]


def _modulation(c, w_ada, b_ada):
    L, D, N = w_ada.shape
    B = c.shape[0]
    tn = 512
    return pl.pallas_call(
        _mod_kernel,
        grid=(L, N // tn),
        in_specs=[pl.BlockSpec((B, D), lambda l, n: (0, 0)),
                  pl.BlockSpec((1, D, tn), lambda l, n: (l, 0, n)),
                  pl.BlockSpec((1, 1, tn), lambda l, n: (l, 0, n))],
        out_specs=pl.BlockSpec((1, B, tn), lambda l, n: (l, 0, n)),
        out_shape=jax.ShapeDtypeStruct((L, B, N), F32),
        compiler_params=_params("arbitrary", "arbitrary"),
        name="adaln_mod",
    )(c, w_ada, b_ada.reshape(L, 1, N))


def _norm_proj_kernel(x_ref, g_ref, sc_ref, sh_ref, w_ref, o_ref, h_ref):
    @pl.when(pl.program_id(1) == 0)
    def _():
        x = x_ref[...]
        y = x * lax.rsqrt(jnp.mean(x * x, axis=-1, keepdims=True) + EPS)
        h = (y * g_ref[...]) * (1.0 + sc_ref[0]) + sh_ref[0]
        h_ref[...] = h.astype(BF16)

    o_ref[...] = jnp.dot(h_ref[...], w_ref[...], preferred_element_type=F32).astype(o_ref.dtype)


def _norm_proj(x2, g, scale, shift, w, S, out_dtype, tm, tn, name):
    M, D = x2.shape
    N = w.shape[1]
    per_b = S // tm
    return pl.pallas_call(
        _norm_proj_kernel,
        grid=(M // tm, N // tn),
        in_specs=[pl.BlockSpec((tm, D), lambda m, n: (m, 0)),
                  pl.BlockSpec((1, D), lambda m, n: (0, 0)),
                  pl.BlockSpec((1, 1, D), lambda m, n: (m // per_b, 0, 0)),
                  pl.BlockSpec((1, 1, D), lambda m, n: (m // per_b, 0, 0)),
                  pl.BlockSpec((D, tn), lambda m, n: (0, n))],
        out_specs=pl.BlockSpec((tm, tn), lambda m, n: (m, n)),
        out_shape=jax.ShapeDtypeStruct((M, N), out_dtype),
        scratch_shapes=[pltpu.VMEM((tm, D), BF16)],
        compiler_params=_params("arbitrary", "arbitrary"),
        name=name,
    )(x2, g, scale, shift, w)


def _store_transposed(x, out_ref, lead=()):
    x_t = x.astype(F32).T.astype(out_ref.dtype)
    chunk = out_ref.shape[-1]
    for i in range(x.shape[0] // chunk):
        out_ref[lead + (i,)] = x_t[:, i * chunk:(i + 1) * chunk]


def _prep_kernel(misc_ref, ckv_ref, ik_ref, ikr_ref, cos_ref, sin_ref, bf_ref, gkv_ref, wkv_ref, fv_ref, sv_ref,
                 fcol_ref, kk_ref, vvt_ref, ikk_ref, fvt_ref, svt_ref, carry_ref):
    @pl.when(pl.program_id(1) == 0)
    def _():
        carry_ref[...] = jnp.zeros_like(carry_ref)

    for p in range(N_HEADS // 2):
        _store_transposed(fv_ref[0, :, p * LANES:(p + 1) * LANES], fvt_ref, (0, p))
        _store_transposed(sv_ref[0, :, p * LANES:(p + 1) * LANES], svt_ref, (0, p))

    tp = misc_ref.shape[1]
    xf = misc_ref[0] + bf_ref[...]
    logf = jnp.minimum(xf, 0.0) - jnp.log1p(jnp.exp(-jnp.abs(xf)))
    r = lax.broadcasted_iota(I32, (tp, tp), 0)
    c = lax.broadcasted_iota(I32, (tp, tp), 1)
    tri = jnp.where(r >= c, 1.0, 0.0).astype(F32)
    cum = jnp.dot(tri, logf, preferred_element_type=F32, precision=HIGHEST) + carry_ref[...]
    carry_ref[...] = cum[tp - 1:tp, :]
    fcol_ref[0] = cum[:, :N_HEADS]

    cos = cos_ref[0]
    sin = sin_ref[0]
    ckv = ckv_ref[0]
    nrm = ckv * lax.rsqrt(jnp.mean(ckv * ckv, axis=-1, keepdims=True) + EPS) * gkv_ref[...]
    kv = jnp.dot(nrm.astype(BF16), wkv_ref[...], preferred_element_type=F32)
    kk_ref[0] = (kv[:, :LANES] * cos + kv[:, LANES:2 * LANES] * sin).astype(BF16)
    _store_transposed(kv[:, 2 * LANES:], vvt_ref, (0,))
    ikk_ref[0] = (ik_ref[0] * cos + ikr_ref[0] * sin).astype(BF16)


def _prep(zs, zm, cos, sin, bf, gkv, wkv, tp, tk_dsa, tk_fox, tk_sb):
    B, S, _ = zs.shape
    assert tp % tk_dsa == 0 and tp % tk_fox == 0 and tp % tk_sb == 0
    blk = lambda col: pl.BlockSpec((1, tp, LANES), lambda b, j: (b, j, col))
    row = pl.BlockSpec((1, tp, LANES), lambda b, j: (b, j, 0))
    wide = lambda col: pl.BlockSpec((1, tp, WIDTH), lambda b, j: (b, j, col))
    vt_spec = lambda tk: pl.BlockSpec((1, N_HEADS // 2, tp // tk, LANES, tk), lambda b, j: (b, 0, j, 0, 0))
    vt_shape = lambda tk: jax.ShapeDtypeStruct((B, N_HEADS // 2, S // tk, LANES, tk), BF16)
    return pl.pallas_call(
        _prep_kernel,
        grid=(B, S // tp),
        in_specs=[blk(_MISC), blk(_CKV), blk(_IK), blk(_IKR), row, row,
                  pl.BlockSpec((1, LANES), lambda b, j: (0, 0)),
                  pl.BlockSpec((1, LANES), lambda b, j: (0, 0)),
                  pl.BlockSpec((KV_RANK, 3 * LANES), lambda b, j: (0, 0)),
                  wide(_FV), wide(_SV)],
        out_specs=[pl.BlockSpec((1, tp, N_HEADS), lambda b, j: (b, j, 0)),
                   row, pl.BlockSpec((1, tp // tk_dsa, LANES, tk_dsa), lambda b, j: (b, j, 0, 0)), row,
                   vt_spec(tk_fox), vt_spec(tk_sb)],
        out_shape=[jax.ShapeDtypeStruct((B, S, N_HEADS), F32),
                   jax.ShapeDtypeStruct((B, S, LANES), BF16),
                   jax.ShapeDtypeStruct((B, S // tk_dsa, LANES, tk_dsa), BF16),
                   jax.ShapeDtypeStruct((B, S, LANES), BF16),
                   vt_shape(tk_fox), vt_shape(tk_sb)],
        scratch_shapes=[pltpu.VMEM((1, LANES), F32)],
        compiler_params=_params("arbitrary", "arbitrary"),
        name="prep",
    )(zs, zs, zs, zs, cos, sin, bf, gkv, wkv, zm, zm)


def _head_masks():
    lane = lax.broadcasted_iota(I32, (1, LANES), 1)
    lo = lane < HEAD_DIM
    return lo, jnp.logical_not(lo)


def _silu(g):
    return g * jax.nn.sigmoid(g)


def _pair_queries_t(q, masks):
    q = q.astype(F32)
    return jnp.concatenate([jnp.where(masks[hh], q, 0.0).T.astype(BF16) for hh in range(2)], axis=1)


def _pair_output(out0_t, out1_t, g):
    return jnp.concatenate([out0_t, out1_t], axis=0).T * _silu(g.astype(F32))


def _pipelined(n, produce, consume, carry):
    def pair(i, carry):
        produce(2 * i + 1, 1)
        carry = consume(2 * i, 0, carry, False)
        produce(2 * i + 2, 0)
        return consume(2 * i + 1, 1, carry, False)

    def odd_tail(carry):
        return consume(n - 1, 0, carry, True)

    def even_tail(carry):
        produce(n - 1, 1)
        carry = consume(n - 2, 0, carry, False)
        return consume(n - 1, 1, carry, True)

    produce(0, 0)
    carry = lax.fori_loop(0, (n - 1) // 2, pair, carry)
    return lax.cond(n % 2 == 1, odd_tail, even_tail, carry)


def _fox_kernel(q_ref, k_ref, vt_ref, g_ref, fcol_ref, o_ref, s_ref, *, t):
    qi = pl.program_id(2)
    qs_t = _pair_queries_t(q_ref[0], _head_masks())
    key_idx0 = lax.broadcasted_iota(I32, (t, t), 0)
    query_idx0 = lax.broadcasted_iota(I32, (t, t), 1)
    f0 = fcol_ref[0, 0, pl.ds(pl.multiple_of(qi * t, t), 1), :]

    def scores(j, slot):
        start = pl.multiple_of(j * t, t)
        b = (f0 - fcol_ref[0, 0, pl.ds(start, t), :]) * LOG2E
        bias = jnp.concatenate([jnp.broadcast_to(b[:, hh:hh + 1], (t, t)) for hh in range(2)], axis=1)
        s_ref[slot] = jnp.dot(k_ref[0, pl.ds(start, t), :], qs_t, preferred_element_type=F32) + bias

    def consume(j, slot, carry, diagonal):
        m, l, acc0, acc1 = carry
        s = s_ref[slot]
        if diagonal:
            valid = key_idx0 <= query_idx0
            s = jnp.where(jnp.concatenate([valid, valid], axis=1), s, NEG)
        m_new = jnp.maximum(m, jnp.max(s, axis=0, keepdims=True))
        alpha = jnp.exp2(m - m_new)
        p = jnp.exp2(s - m_new)
        l = alpha * l + jnp.sum(p, axis=0, keepdims=True)
        p = p.astype(BF16)
        v_t = vt_ref[0, 0, j]
        acc0 = alpha[:, :t] * acc0 + jnp.dot(v_t[:HEAD_DIM], p[:, :t], preferred_element_type=F32)
        acc1 = alpha[:, t:] * acc1 + jnp.dot(v_t[HEAD_DIM:], p[:, t:], preferred_element_type=F32)
        return m_new, l, acc0, acc1

    acc = jnp.zeros((HEAD_DIM, t), F32)
    init = (jnp.full((1, 2 * t), NEG, F32), jnp.zeros((1, 2 * t), F32), acc, acc)
    _, l, acc0, acc1 = _pipelined(qi + 1, scores, consume, init)
    o_ref[0] = _pair_output(acc0 / l[:, :t], acc1 / l[:, t:], g_ref[0]).astype(o_ref.dtype)


def _fox(zm, fcol, fvt, tq):
    B, S, _ = zm.shape
    tk = fvt.shape[4]
    assert tq == tk
    col = lambda base: (lambda b, h, i: (b, i, base * 4 + h))
    f4 = fcol.reshape(B, S, 4, 2).transpose(0, 2, 1, 3)
    return pl.pallas_call(
        functools.partial(_fox_kernel, t=tq),
        grid=(B, 4, S // tq),
        in_specs=[pl.BlockSpec((1, tq, LANES), col(_FQ)),
                  pl.BlockSpec((1, S, LANES), lambda b, h, i: (b, 0, _FK * 4 + h)),
                  pl.BlockSpec((1, 1, S // tk, LANES, tk), lambda b, h, i: (b, h, 0, 0, 0)),
                  pl.BlockSpec((1, tq, LANES), col(_FG)),
                  pl.BlockSpec((1, 1, S, 2), lambda b, h, i: (b, h, 0, 0))],
        out_specs=pl.BlockSpec((1, tq, LANES), lambda b, h, i: (b, i, h)),
        out_shape=jax.ShapeDtypeStruct((B, S, WIDTH), BF16),
        scratch_shapes=[pltpu.VMEM((2, tk, 2 * tq), F32)],
        compiler_params=_params("arbitrary", "arbitrary", "arbitrary"),
        name="fox_attention",
    )(zm, zm, fvt, zm, f4)


def _sb_kernel(q_ref, k_ref, vt_ref, g_ref, sfx_ref, o_ref, z_ref, *, tq, tk):
    assert tq == 2 * tk
    qi = pl.program_id(2)
    n_off = 2 * qi
    qs_t = _pair_queries_t(q_ref[0], _head_masks())
    key_idx0 = lax.broadcasted_iota(I32, (tk, tq), 0)
    query_idx0 = lax.broadcasted_iota(I32, (tk, tq), 1)

    def logits(u, slot):
        c = jnp.maximum(n_off + 1 - u, 0)
        z_ref[slot] = jnp.dot(k_ref[0, pl.ds(pl.multiple_of(c * tk, tk), tk), :], qs_t, preferred_element_type=F32)

    blk = sfx_ref.shape[0]

    def consume(u, slot, carry, masked):
        later, acc0, acc1 = carry
        c = n_off + 1 - u
        z = z_ref[slot]
        sp = jnp.maximum(z, 0.0) + jnp.log(1.0 + jnp.exp2(-jnp.abs(z))) * LOG2E
        if masked:
            valid = (c - n_off) * tk + key_idx0 < query_idx0
            valid = jnp.concatenate([valid, valid], axis=1)
            sp = jnp.where(valid, sp, 0.0)
        hi = sp.astype(BF16)
        lo = (sp - hi.astype(F32)).astype(BF16)
        parts, above = [], None
        for b in reversed(range(tk // blk)):
            rows = slice(b * blk, (b + 1) * blk)
            inc = jnp.dot(sfx_ref[...], jnp.concatenate([hi[rows], lo[rows]], axis=0), preferred_element_type=F32)
            inc = inc if above is None else inc + above
            parts.insert(0, inc)
            above = inc[:1, :]
        incl = jnp.concatenate(parts, axis=0)
        a = jnp.exp2(z - incl - later)
        if masked:
            a = jnp.where(valid, a, 0.0)
        a = a.astype(BF16)
        v_t = vt_ref[0, 0, c]
        acc0 = acc0 + jnp.dot(v_t[:HEAD_DIM], a[:, :tq], preferred_element_type=F32)
        acc1 = acc1 + jnp.dot(v_t[HEAD_DIM:], a[:, tq:], preferred_element_type=F32)
        return later + above, acc0, acc1

    def pair(i, carry, masked):
        logits(2 * i + 1, 1)
        carry = consume(2 * i, 0, carry, masked)
        logits(2 * i + 2, 0)
        return consume(2 * i + 1, 1, carry, masked)

    acc = jnp.zeros((HEAD_DIM, tq), F32)
    logits(0, 0)
    carry = pair(0, (jnp.zeros((1, 2 * tq), F32), acc, acc), True)
    _, acc0, acc1 = lax.fori_loop(1, qi + 1, functools.partial(pair, masked=False), carry)
    o_ref[0] = _pair_output(acc0, acc1, g_ref[0]).astype(o_ref.dtype)


def _sb(zm, svt, tq):
    B, S, _ = zm.shape
    tk = svt.shape[4]
    assert tq % tk == 0
    col = lambda base: (lambda b, h, i: (b, i, base * 4 + h))
    blk = LANES
    j = jnp.arange(2 * blk, dtype=I32)[None, :] % blk
    sfx = (j >= jnp.arange(blk, dtype=I32)[:, None]).astype(BF16)
    return pl.pallas_call(
        functools.partial(_sb_kernel, tq=tq, tk=tk),
        grid=(B, 4, S // tq),
        in_specs=[pl.BlockSpec((1, tq, LANES), col(_SQ)),
                  pl.BlockSpec((1, S, LANES), lambda b, h, i: (b, 0, _SK * 4 + h)),
                  pl.BlockSpec((1, 1, S // tk, LANES, tk), lambda b, h, i: (b, h, 0, 0, 0)),
                  pl.BlockSpec((1, tq, LANES), col(_SG)),
                  pl.BlockSpec((blk, 2 * blk), lambda b, h, i: (0, 0))],
        out_specs=pl.BlockSpec((1, tq, LANES), lambda b, h, i: (b, i, h)),
        out_shape=jax.ShapeDtypeStruct((B, S, WIDTH), BF16),
        scratch_shapes=[pltpu.VMEM((2, tk, 2 * tq), F32)],
        compiler_params=_params("arbitrary", "arbitrary", "arbitrary"),
        name="stick_breaking_attention",
    )(zm, zm, svt, zm, sfx)


def _dsa_kernel(iq_ref, iqr_ref, misc_ref, cos_ref, sin_ref, dq_ref, dqr_ref, dg_ref,
                ikk_ref, kk_ref, vvt_ref, o_ref, key_ref, bias_ref, cut_ref, s_ref, d_ref, *, tq, tk, ktop, seq_bits):
    qi = pl.program_id(1)
    q0 = qi * tq
    nch = (q0 + tq + tk - 1) // tk
    masks = _head_masks()
    cos = cos_ref[0]
    sin = sin_ref[0]
    cos4 = jnp.concatenate([cos] * 4, axis=1)
    sin4 = jnp.concatenate([sin] * 4, axis=1)
    iq = iq_ref[0] * cos4 + iqr_ref[0] * sin4
    qd = dq_ref[0].astype(F32) * cos4 + dqr_ref[0].astype(F32) * sin4

    def slab_t(x, h):
        blk = x[:, (h // 2) * LANES:(h // 2 + 1) * LANES]
        return jnp.where(masks[h % 2], blk, 0.0).T.astype(BF16)

    iq_t = [jnp.concatenate([slab_t(iq, 2 * p), slab_t(iq, 2 * p + 1)], axis=1) for p in range(N_HEADS // 2)]
    qs_t = jnp.concatenate([slab_t(qd, h) for h in range(N_HEADS)], axis=1)
    w_t = misc_ref[0].T[N_HEADS:2 * N_HEADS, :] * (N_HEADS ** -0.5)
    w_pair = [jnp.concatenate([w_t[2 * p:2 * p + 1], w_t[2 * p + 1:2 * p + 2]], axis=1) for p in range(N_HEADS // 2)]

    key_idx0 = lax.broadcasted_iota(I32, (tk, tq), 0)
    query_idx = q0 + lax.broadcasted_iota(I32, (tk, tq), 1)

    def index_dots(c, slot):
        ik = ikk_ref[0, pl.ds(pl.multiple_of(c * tk, tk), tk), :]
        for p in range(N_HEADS // 2):
            d_ref[slot, p] = jnp.dot(ik, iq_t[p], preferred_element_type=F32)

    def score_chunk(c, slot, carry, last):
        acc2 = jnp.zeros((tk, 2 * tq), F32)
        for p in range(N_HEADS // 2):
            acc2 = acc2 + jnp.maximum(d_ref[slot, p], 0.0) * w_pair[p]
        acc = acc2[:, :tq] + acc2[:, tq:]
        acc = jnp.where(acc == 0.0, 0.0, acc)
        bits = lax.bitcast_convert_type(acc, I32)
        key = bits ^ (lax.shift_right_arithmetic(bits, 31) & 0x7FFFFFFF)
        key = jnp.maximum(key, INT_MIN + 1)
        key_ref[c] = jnp.where(c * tk + key_idx0 <= query_idx, key, INT_MIN)
        return carry

    _pipelined(nch, index_dots, score_chunk, 0)

    n_acc = 4

    slab = lambda row: jnp.broadcast_to(row, (tk, tq))

    def count(pred):
        def body(c, cnt):
            hit = jnp.where(pred(key_ref[c], c * tk + key_idx0), 1.0, 0.0)
            return cnt + hit.reshape(n_acc, tk // (8 * n_acc), 8, tq).sum(axis=1)
        cnt = lax.fori_loop(0, nch, body, jnp.zeros((n_acc, 8, tq), F32))
        return jnp.sum(cnt.reshape(n_acc * 8, tq), axis=0, keepdims=True)

    def bit_body(b, prefix):
        cand_u = prefix | lax.shift_left(jnp.int32(1), 31 - b)
        cand = slab(cand_u ^ INT_MIN)
        n = count(lambda key, idx: key >= cand)
        return jnp.where(n >= ktop, cand_u, prefix)

    prefix = lax.fori_loop(0, 32, bit_body, jnp.zeros((1, tq), I32))
    thr_s = slab(prefix ^ INT_MIN)
    n_gt = count(lambda key, idx: key > thr_s)
    n_ge = count(lambda key, idx: key >= thr_s)
    need = ktop - n_gt

    cut_ref[...] = jnp.full(cut_ref.shape, 2 ** seq_bits, I32)

    @pl.when(jnp.max(n_ge) > ktop)
    def _():
        def idx_body(b, lo):
            cand_i = lo | lax.shift_left(jnp.int32(1), seq_bits - 1 - b)
            cand = slab(cand_i)
            g = count(lambda key, idx: (key == thr_s) & (idx < cand))
            return jnp.where(g < need, cand_i, lo)
        lo = lax.fori_loop(0, seq_bits, idx_body, jnp.zeros((1, tq), I32))
        cut_ref[...] = jnp.broadcast_to(lo, cut_ref.shape)

    thr = thr_s
    cut = slab(cut_ref[:1, :])

    def bias_chunk(c, carry):
        key = key_ref[c]
        sel = (key > thr) | ((key == thr) & (c * tk + key_idx0 <= cut))
        sel = sel & (key > INT_MIN)
        bias_ref[c] = jnp.where(sel, 0.0, NEG)
        return carry

    lax.fori_loop(0, nch, bias_chunk, 0)

    def logits(c, slot):
        k = kk_ref[0, pl.ds(pl.multiple_of(c * tk, tk), tk), :]
        s = jnp.dot(k, qs_t, preferred_element_type=F32)
        s_ref[slot] = s + jnp.concatenate([bias_ref[c]] * N_HEADS, axis=1)

    def consume(c, slot, carry, last):
        m, l, acc = carry
        s = s_ref[slot]
        m_new = jnp.maximum(m, jnp.max(s, axis=0, keepdims=True))
        alpha = jnp.exp2(m - m_new)
        pr = jnp.exp2(s - m_new)
        l = alpha * l + jnp.sum(pr, axis=0, keepdims=True)
        v_t = vvt_ref[0, c][:HEAD_DIM]
        acc = alpha * acc + jnp.dot(v_t, pr.astype(BF16), preferred_element_type=F32)
        return m_new, l, acc

    wide = N_HEADS * tq
    init = (jnp.full((1, wide), NEG, F32), jnp.zeros((1, wide), F32), jnp.zeros((HEAD_DIM, wide), F32))
    _, l, acc = _pipelined(nch, logits, consume, init)
    out_t = acc / l
    for p in range(N_HEADS // 2):
        g = dg_ref[0, :, p * LANES:(p + 1) * LANES]
        y = _pair_output(out_t[:, 2 * p * tq:(2 * p + 1) * tq], out_t[:, (2 * p + 1) * tq:(2 * p + 2) * tq], g)
        o_ref[0, :, p * LANES:(p + 1) * LANES] = y.astype(o_ref.dtype)


def _dsa(zs, zm, cos, sin, ikk, kk, vvt, tq, ktop):
    B, S, _ = zs.shape
    tk = vvt.shape[3]
    assert tq % LANES == 0
    wide = lambda col: pl.BlockSpec((1, tq, WIDTH), lambda b, i: (b, i, col))
    row = pl.BlockSpec((1, tq, LANES), lambda b, i: (b, i, 0))
    seq = pl.BlockSpec((1, S, LANES), lambda b, i: (b, 0, 0))
    seq_bits = max(1, (S - 1).bit_length())
    return pl.pallas_call(
        functools.partial(_dsa_kernel, tq=tq, tk=tk, ktop=ktop, seq_bits=seq_bits),
        grid=(B, S // tq),
        in_specs=[wide(_IQ), wide(_IQR),
                  pl.BlockSpec((1, tq, LANES), lambda b, i: (b, i, _MISC)),
                  row, row, wide(_DQ), wide(_DQR), wide(_DG), seq, seq,
                  pl.BlockSpec((1, S // tk, LANES, tk), lambda b, i: (b, 0, 0, 0))],
        out_specs=pl.BlockSpec((1, tq, WIDTH), lambda b, i: (b, i, 0)),
        out_shape=jax.ShapeDtypeStruct((B, S, WIDTH), BF16),
        scratch_shapes=[pltpu.VMEM((S // tk, tk, tq), I32),
                        pltpu.VMEM((S // tk, tk, tq), F32),
                        pltpu.VMEM((8, tq), I32),
                        pltpu.VMEM((2, tk, N_HEADS * tq), F32),
                        pltpu.VMEM((2, N_HEADS // 2, tk, 2 * tq), F32)],
        compiler_params=_params("arbitrary", "arbitrary"),
        name="dsa_attention",
    )(zs, zs, zs, cos, sin, zm, zm, zm, ikk, kk, vvt)


def _merge_kernel(x_ref, gate_ref, yf_ref, yd_ref, ys_ref, m0_ref, m1_ref, m2_ref,
                  wf_ref, wd_ref, ws_ref, wo_ref, gfin_ref, o_ref, *, final):
    def branch(y_ref, w_ref, m_ref):
        proj = jnp.dot(y_ref[...], w_ref[...], preferred_element_type=F32)
        return jax.nn.sigmoid(m_ref[...].astype(F32)) * proj

    mixed = branch(yf_ref, wf_ref, m0_ref) + branch(yd_ref, wd_ref, m1_ref) + branch(ys_ref, ws_ref, m2_ref)
    o = x_ref[...] + gate_ref[0] * jnp.dot(mixed.astype(BF16), wo_ref[...], preferred_element_type=F32)
    if final:
        o = o * lax.rsqrt(jnp.mean(o * o, axis=-1, keepdims=True) + EPS) * gfin_ref[...]
    o_ref[...] = o


def _merge(x2, gate, yf, yd, ys, zm2, wf, wd, ws, wo, gfin, S, tm, final):
    M, D = x2.shape
    per_b = S // tm
    ysp = pl.BlockSpec((tm, WIDTH), lambda m: (m, 0))
    msp = lambda j: pl.BlockSpec((tm, D), lambda m: (m, j))
    wsp = pl.BlockSpec((WIDTH, D), lambda m: (0, 0))
    return pl.pallas_call(
        functools.partial(_merge_kernel, final=final),
        grid=(M // tm,),
        in_specs=[pl.BlockSpec((tm, D), lambda m: (m, 0)),
                  pl.BlockSpec((1, 1, D), lambda m: (m // per_b, 0, 0)),
                  ysp, ysp, ysp, msp(0), msp(1), msp(2), wsp, wsp, wsp,
                  pl.BlockSpec((D, D), lambda m: (0, 0)),
                  pl.BlockSpec((1, D), lambda m: (0, 0))],
        out_specs=pl.BlockSpec((tm, D), lambda m: (m, 0)),
        out_shape=jax.ShapeDtypeStruct((M, D), F32),
        compiler_params=_params("arbitrary"),
        name="merge_out",
    )(x2, gate, yf, yd, ys, zm2, zm2, zm2, wf, wd, ws, wo, gfin)


def _rot_cols(w):
    d, n = w.shape
    w4 = w.reshape(d, n // HEAD_DIM, 2, HEAD_DIM // 2)
    return jnp.concatenate([-w4[:, :, 1:], w4[:, :, :1]], axis=2).reshape(d, n)


def _split_w_in(w):
    sizes = (WIDTH, WIDTH, WIDTH, N_HEADS, WIDTH,
             WIDTH, KV_RANK, WIDTH, HEAD_DIM, N_HEADS, WIDTH,
             WIDTH, WIDTH, WIDTH, WIDTH, w.shape[1] - (11 * WIDTH + 2 * N_HEADS + KV_RANK + HEAD_DIM))
    parts, off = [], 0
    for s in sizes:
        parts.append(w[:, off:off + s])
        off += s
    return parts


def _layer_weights(w_in, w_kv_up):
    (fq, fk, fv, ff, fg, dq, dckv, diq, dik, diw, dg, sq, sk, sv, sg, merge) = _split_w_in(w_in)
    scale = HEAD_DIM ** -0.5
    fq, dq, sq, diq = fq * (scale * LOG2E), dq * (scale * LOG2E), sq * (scale * LOG2E), diq * scale
    w_main = jnp.concatenate([merge, fq, fk, fv, fg, dq, _rot_cols(dq), dg, sq, sk, sv, sg], axis=1)
    pad = jnp.zeros((w_in.shape[0], LANES - 2 * N_HEADS), w_in.dtype)
    dikr = _rot_cols(dik)
    w_small = jnp.concatenate([diq, _rot_cols(diq), ff, diw, pad, dckv, dik, dik, dikr, dikr], axis=1)
    wk, wv = w_kv_up[:, :HEAD_DIM], w_kv_up[:, HEAD_DIM:]
    wkr = _rot_cols(wk)
    w_kv = jnp.concatenate([wk, wk, wkr, wkr, wv, wv], axis=1)
    return w_main.astype(BF16), w_small.astype(BF16), w_kv.astype(BF16)


def kernel(x, c, positions, w_ada, b_ada, g_norm, w_in, b_fgt, g_kv, w_kv_up, w_br_fox, w_br_dsa, w_br_sb, w_out, g_final):
    B, S, D = x.shape
    depth = w_in.shape[0]
    assert D == 2 * WIDTH and S % 256 == 0
    ktop = min(TOPK_MAX, S // 4)
    tm = min(1024, S)
    tp = min(512, S)
    tq_att = min(512, S)
    tk_dsa = tk_fox = tp
    tk_sb = 256

    half = HEAD_DIM // 2
    inv_freq = ROPE_THETA ** (-jnp.arange(half, dtype=F32) / half)
    ang = positions.astype(F32)[..., None] * inv_freq
    cos = jnp.tile(jnp.cos(ang), (1, 1, LANES // half))
    sin = jnp.tile(jnp.sin(ang), (1, 1, LANES // half))

    mod = _modulation(c, w_ada, b_ada)
    x2 = x.reshape(B * S, D)
    for l in range(depth):
        shift = mod[l, :, None, :D]
        scale = mod[l, :, None, D:2 * D]
        gate = mod[l, :, None, 2 * D:]
        w_main, w_small, w_kv = _layer_weights(w_in[l], w_kv_up[l])
        g = g_norm[l].reshape(1, D)
        zm2 = _norm_proj(x2, g, scale, shift, w_main, S, BF16, tm, 512, "norm_proj_main")
        zs2 = _norm_proj(x2, g, scale, shift, w_small, S, F32, tm, 512, "norm_proj_small")
        zm = zm2.reshape(B, S, N_MAIN)
        zs = zs2.reshape(B, S, N_SMALL)
        bf = jnp.zeros((1, LANES), F32).at[0, :N_HEADS].set(b_fgt[l])
        fcol, kk, vvt, ikk, fvt, svt = _prep(zs, zm, cos, sin, bf, g_kv[l].reshape(1, KV_RANK), w_kv,
                                             tp, tk_dsa, tk_fox, tk_sb)
        y_fox = _fox(zm, fcol, fvt, tq_att)
        y_sb = _sb(zm, svt, tq_att)
        y_dsa = _dsa(zs, zm, cos, sin, ikk, kk, vvt, LANES, ktop)
        x2 = _merge(x2, gate, y_fox.reshape(B * S, WIDTH), y_dsa.reshape(B * S, WIDTH),
                    y_sb.reshape(B * S, WIDTH), zm2,
                    w_br_fox[l].astype(BF16), w_br_dsa[l].astype(BF16), w_br_sb[l].astype(BF16),
                    w_out[l].astype(BF16), g_final.reshape(1, D), S, min(512, S), l == depth - 1)
    return x2.reshape(B, S, D)
```

```python
import functools

import jax
import jax.numpy as jnp
from jax import lax
from jax.experimental import pallas as pl
from jax.experimental.pallas import tpu as pltpu

F32 = jnp.float32
BF16 = jnp.bfloat16
I32 = jnp.int32
HIGHEST = lax.Precision.HIGHEST

HEAD_DIM = 64
N_HEADS = 8
WIDTH = N_HEADS * HEAD_DIM
KV_RANK = 128
TOPK_MAX = 256
ROPE_THETA = 10000.0
EPS = 1e-6
LANES = 128
NEG = -1e30
INT_MIN = -(2 ** 31)
LOG2E = 1.4426950408889634
VMEM_LIMIT = 56 * 1024 * 1024

_MERGE, _FQ, _FK, _FV, _FG, _DQ, _DQR, _DG, _SQ, _SK, _SV, _SG = 0, 6, 7, 8, 9, 10, 11, 12, 13, 14, 15, 16
N_MAIN = 17 * WIDTH
_IQ, _IQR = 0, 1
_MISC, _CKV, _IK, _IKR = 8, 9, 10, 11
N_SMALL = 12 * LANES


def _params(*sem):
    return pltpu.CompilerParams(dimension_semantics=sem, vmem_limit_bytes=VMEM_LIMIT)


def _mod_kernel(c_ref, w_ref, b_ref, o_ref):
    c = c_ref[...]
    sc = c * jax.nn.sigmoid(c)
    o_ref[0] = jnp.dot(sc, w_ref[0], preferred_element_type=F32, precision=HIGHEST) + b_ref[0]


def _modulation(c, w_ada, b_ada):
    L, D, N = w_ada.shape
    B = c.shape[0]
    tn = 512
    return pl.pallas_call(
        _mod_kernel,
        grid=(L, N // tn),
        in_specs=[pl.BlockSpec((B, D), lambda l, n: (0, 0)),
                  pl.BlockSpec((1, D, tn), lambda l, n: (l, 0, n)),
                  pl.BlockSpec((1, 1, tn), lambda l, n: (l, 0, n))],
        out_specs=pl.BlockSpec((1, B, tn), lambda l, n: (l, 0, n)),
        out_shape=jax.ShapeDtypeStruct((L, B, N), F32),
        compiler_params=_params("arbitrary", "arbitrary"),
        name="adaln_mod",
    )(c, w_ada, b_ada.reshape(L, 1, N))


def _norm_proj_kernel(x_ref, g_ref, sc_ref, sh_ref, w_ref, o_ref, h_ref):
    @pl.when(pl.program_id(1) == 0)
    def _():
        x = x_ref[...]
        y = x * lax.rsqrt(jnp.mean(x * x, axis=-1, keepdims=True) + EPS)
        h = (y * g_ref[...]) * (1.0 + sc_ref[0]) + sh_ref[0]
        h_ref[...] = h.astype(BF16)

    o_ref[...] = jnp.dot(h_ref[...], w_ref[...], preferred_element_type=F32).astype(o_ref.dtype)


def _norm_proj(x2, g, scale, shift, w, S, out_dtype, tm, tn, name):
    M, D = x2.shape
    N = w.shape[1]
    per_b = S // tm
    return pl.pallas_call(
        _norm_proj_kernel,
        grid=(M // tm, N // tn),
        in_specs=[pl.BlockSpec((tm, D), lambda m, n: (m, 0)),
                  pl.BlockSpec((1, D), lambda m, n: (0, 0)),
                  pl.BlockSpec((1, 1, D), lambda m, n: (m // per_b, 0, 0)),
                  pl.BlockSpec((1, 1, D), lambda m, n: (m // per_b, 0, 0)),
                  pl.BlockSpec((D, tn), lambda m, n: (0, n))],
        out_specs=pl.BlockSpec((tm, tn), lambda m, n: (m, n)),
        out_shape=jax.ShapeDtypeStruct((M, N), out_dtype),
        scratch_shapes=[pltpu.VMEM((tm, D), BF16)],
        compiler_params=_params("arbitrary", "arbitrary"),
        name=name,
    )(x2, g, scale, shift, w)


def _store_transposed(x, out_ref, lead=()):
    x_t = x.astype(F32).T.astype(out_ref.dtype)
    chunk = out_ref.shape[-1]
    for i in range(x.shape[0] // chunk):
        out_ref[lead + (i,)] = x_t[:, i * chunk:(i + 1) * chunk]


def _prep_kernel(misc_ref, ckv_ref, ik_ref, ikr_ref, cos_ref, sin_ref, bf_ref, gkv_ref, wkv_ref, fv_ref, sv_ref,
                 fcol_ref, kk_ref, vvt_ref, ikk_ref, fvt_ref, svt_ref, carry_ref):
    @pl.when(pl.program_id(1) == 0)
    def _():
        carry_ref[...] = jnp.zeros_like(carry_ref)

    for p in range(N_HEADS // 2):
        _store_transposed(fv_ref[0, :, p * LANES:(p + 1) * LANES], fvt_ref, (0, p))
        _store_transposed(sv_ref[0, :, p * LANES:(p + 1) * LANES], svt_ref, (0, p))

    tp = misc_ref.shape[1]
    xf = misc_ref[0] + bf_ref[...]
    logf = jnp.minimum(xf, 0.0) - jnp.log1p(jnp.exp(-jnp.abs(xf)))
    r = lax.broadcasted_iota(I32, (tp, tp), 0)
    c = lax.broadcasted_iota(I32, (tp, tp), 1)
    tri = jnp.where(r >= c, 1.0, 0.0).astype(F32)
    cum = jnp.dot(tri, logf, preferred_element_type=F32, precision=HIGHEST) + carry_ref[...]
    carry_ref[...] = cum[tp - 1:tp, :]
    fcol_ref[0] = cum[:, :N_HEADS]

    cos = cos_ref[0]
    sin = sin_ref[0]
    ckv = ckv_ref[0]
    nrm = ckv * lax.rsqrt(jnp.mean(ckv * ckv, axis=-1, keepdims=True) + EPS) * gkv_ref[...]
    kv = jnp.dot(nrm.astype(BF16), wkv_ref[...], preferred_element_type=F32)
    kk_ref[0] = (kv[:, :LANES] * cos + kv[:, LANES:2 * LANES] * sin).astype(BF16)
    _store_transposed(kv[:, 2 * LANES:], vvt_ref, (0,))
    ikk_ref[0] = (ik_ref[0] * cos + ikr_ref[0] * sin).astype(BF16)


def _prep(zs, zm, cos, sin, bf, gkv, wkv, tp, tk_dsa, tk_fox, tk_sb):
    B, S, _ = zs.shape
    assert tp % tk_dsa == 0 and tp % tk_fox == 0 and tp % tk_sb == 0
    blk = lambda col: pl.BlockSpec((1, tp, LANES), lambda b, j: (b, j, col))
    row = pl.BlockSpec((1, tp, LANES), lambda b, j: (b, j, 0))
    wide = lambda col: pl.BlockSpec((1, tp, WIDTH), lambda b, j: (b, j, col))
    vt_spec = lambda tk: pl.BlockSpec((1, N_HEADS // 2, tp // tk, LANES, tk), lambda b, j: (b, 0, j, 0, 0))
    vt_shape = lambda tk: jax.ShapeDtypeStruct((B, N_HEADS // 2, S // tk, LANES, tk), BF16)
    return pl.pallas_call(
        _prep_kernel,
        grid=(B, S // tp),
        in_specs=[blk(_MISC), blk(_CKV), blk(_IK), blk(_IKR), row, row,
                  pl.BlockSpec((1, LANES), lambda b, j: (0, 0)),
                  pl.BlockSpec((1, LANES), lambda b, j: (0, 0)),
                  pl.BlockSpec((KV_RANK, 3 * LANES), lambda b, j: (0, 0)),
                  wide(_FV), wide(_SV)],
        out_specs=[pl.BlockSpec((1, tp, N_HEADS), lambda b, j: (b, j, 0)),
                   row, pl.BlockSpec((1, tp // tk_dsa, LANES, tk_dsa), lambda b, j: (b, j, 0, 0)), row,
                   vt_spec(tk_fox), vt_spec(tk_sb)],
        out_shape=[jax.ShapeDtypeStruct((B, S, N_HEADS), F32),
                   jax.ShapeDtypeStruct((B, S, LANES), BF16),
                   jax.ShapeDtypeStruct((B, S // tk_dsa, LANES, tk_dsa), BF16),
                   jax.ShapeDtypeStruct((B, S, LANES), BF16),
                   vt_shape(tk_fox), vt_shape(tk_sb)],
        scratch_shapes=[pltpu.VMEM((1, LANES), F32)],
        compiler_params=_params("arbitrary", "arbitrary"),
        name="prep",
    )(zs, zs, zs, zs, cos, sin, bf, gkv, wkv, zm, zm)


def _head_masks():
    lane = lax.broadcasted_iota(I32, (1, LANES), 1)
    lo = lane < HEAD_DIM
    return lo, jnp.logical_not(lo)


def _silu(g):
    return g * jax.nn.sigmoid(g)


def _pair_queries_t(q, masks):
    q = q.astype(F32)
    return jnp.concatenate([jnp.where(masks[hh], q, 0.0).T.astype(BF16) for hh in range(2)], axis=1)


def _pair_output(out0_t, out1_t, g):
    return jnp.concatenate([out0_t, out1_t], axis=0).T * _silu(g.astype(F32))


def _pipelined(n, produce, consume, carry):
    def pair(i, carry):
        produce(2 * i + 1, 1)
        carry = consume(2 * i, 0, carry, False)
        produce(2 * i + 2, 0)
        return consume(2 * i + 1, 1, carry, False)

    def odd_tail(carry):
        return consume(n - 1, 0, carry, True)

    def even_tail(carry):
        produce(n - 1, 1)
        carry = consume(n - 2, 0, carry, False)
        return consume(n - 1, 1, carry, True)

    produce(0, 0)
    carry = lax.fori_loop(0, (n - 1) // 2, pair, carry)
    return lax.cond(n % 2 == 1, odd_tail, even_tail, carry)


def _fox_kernel(q_ref, k_ref, vt_ref, g_ref, fcol_ref, o_ref, s_ref, *, t):
    qi = pl.program_id(2)
    qs_t = _pair_queries_t(q_ref[0], _head_masks())
    key_idx0 = lax.broadcasted_iota(I32, (t, t), 0)
    query_idx0 = lax.broadcasted_iota(I32, (t, t), 1)
    f0 = fcol_ref[0, 0, pl.ds(pl.multiple_of(qi * t, t), 1), :]

    def scores(j, slot):
        start = pl.multiple_of(j * t, t)
        b = (f0 - fcol_ref[0, 0, pl.ds(start, t), :]) * LOG2E
        bias = jnp.concatenate([jnp.broadcast_to(b[:, hh:hh + 1], (t, t)) for hh in range(2)], axis=1)
        s_ref[slot] = jnp.dot(k_ref[0, pl.ds(start, t), :], qs_t, preferred_element_type=F32) + bias

    def consume(j, slot, carry, diagonal):
        m, l, acc0, acc1 = carry
        s = s_ref[slot]
        if diagonal:
            valid = key_idx0 <= query_idx0
            s = jnp.where(jnp.concatenate([valid, valid], axis=1), s, NEG)
        m_new = jnp.maximum(m, jnp.max(s, axis=0, keepdims=True))
        alpha = jnp.exp2(m - m_new)
        p = jnp.exp2(s - m_new)
        l = alpha * l + jnp.sum(p, axis=0, keepdims=True)
        p = p.astype(BF16)
        v_t = vt_ref[0, 0, j]
        acc0 = alpha[:, :t] * acc0 + jnp.dot(v_t[:HEAD_DIM], p[:, :t], preferred_element_type=F32)
        acc1 = alpha[:, t:] * acc1 + jnp.dot(v_t[HEAD_DIM:], p[:, t:], preferred_element_type=F32)
        return m_new, l, acc0, acc1

    acc = jnp.zeros((HEAD_DIM, t), F32)
    init = (jnp.full((1, 2 * t), NEG, F32), jnp.zeros((1, 2 * t), F32), acc, acc)
    _, l, acc0, acc1 = _pipelined(qi + 1, scores, consume, init)
    o_ref[0] = _pair_output(acc0 / l[:, :t], acc1 / l[:, t:], g_ref[0]).astype(o_ref.dtype)


def _fox(zm, fcol, fvt, tq):
    B, S, _ = zm.shape
    tk = fvt.shape[4]
    assert tq == tk
    col = lambda base: (lambda b, h, i: (b, i, base * 4 + h))
    f4 = fcol.reshape(B, S, 4, 2).transpose(0, 2, 1, 3)
    return pl.pallas_call(
        functools.partial(_fox_kernel, t=tq),
        grid=(B, 4, S // tq),
        in_specs=[pl.BlockSpec((1, tq, LANES), col(_FQ)),
                  pl.BlockSpec((1, S, LANES), lambda b, h, i: (b, 0, _FK * 4 + h)),
                  pl.BlockSpec((1, 1, S // tk, LANES, tk), lambda b, h, i: (b, h, 0, 0, 0)),
                  pl.BlockSpec((1, tq, LANES), col(_FG)),
                  pl.BlockSpec((1, 1, S, 2), lambda b, h, i: (b, h, 0, 0))],
        out_specs=pl.BlockSpec((1, tq, LANES), lambda b, h, i: (b, i, h)),
        out_shape=jax.ShapeDtypeStruct((B, S, WIDTH), BF16),
        scratch_shapes=[pltpu.VMEM((2, tk, 2 * tq), F32)],
        compiler_params=_params("arbitrary", "arbitrary", "arbitrary"),
        name="fox_attention",
    )(zm, zm, fvt, zm, f4)


def _sb_kernel(q_ref, k_ref, vt_ref, g_ref, sfx_ref, o_ref, z_ref, *, tq, tk):
    assert tq == 2 * tk
    qi = pl.program_id(2)
    n_off = 2 * qi
    qs_t = _pair_queries_t(q_ref[0], _head_masks())
    key_idx0 = lax.broadcasted_iota(I32, (tk, tq), 0)
    query_idx0 = lax.broadcasted_iota(I32, (tk, tq), 1)

    def logits(u, slot):
        c = jnp.maximum(n_off + 1 - u, 0)
        z_ref[slot] = jnp.dot(k_ref[0, pl.ds(pl.multiple_of(c * tk, tk), tk), :], qs_t, preferred_element_type=F32)

    blk = sfx_ref.shape[0]

    def consume(u, slot, carry, masked):
        later, acc0, acc1 = carry
        c = n_off + 1 - u
        z = z_ref[slot]
        sp = jnp.maximum(z, 0.0) + jnp.log(1.0 + jnp.exp2(-jnp.abs(z))) * LOG2E
        if masked:
            valid = (c - n_off) * tk + key_idx0 < query_idx0
            valid = jnp.concatenate([valid, valid], axis=1)
            sp = jnp.where(valid, sp, 0.0)
        hi = sp.astype(BF16)
        lo = (sp - hi.astype(F32)).astype(BF16)
        parts, above = [], None
        for b in reversed(range(tk // blk)):
            rows = slice(b * blk, (b + 1) * blk)
            inc = jnp.dot(sfx_ref[...], jnp.concatenate([hi[rows], lo[rows]], axis=0), preferred_element_type=F32)
            inc = inc if above is None else inc + above
            parts.insert(0, inc)
            above = inc[:1, :]
        incl = jnp.concatenate(parts, axis=0)
        a = jnp.exp2(z - incl - later)
        if masked:
            a = jnp.where(valid, a, 0.0)
        a = a.astype(BF16)
        v_t = vt_ref[0, 0, c]
        acc0 = acc0 + jnp.dot(v_t[:HEAD_DIM], a[:, :tq], preferred_element_type=F32)
        acc1 = acc1 + jnp.dot(v_t[HEAD_DIM:], a[:, tq:], preferred_element_type=F32)
        return later + above, acc0, acc1

    def pair(i, carry, masked):
        logits(2 * i + 1, 1)
        carry = consume(2 * i, 0, carry, masked)
        logits(2 * i + 2, 0)
        return consume(2 * i + 1, 1, carry, masked)

    acc = jnp.zeros((HEAD_DIM, tq), F32)
    logits(0, 0)
    carry = pair(0, (jnp.zeros((1, 2 * tq), F32), acc, acc), True)
    _, acc0, acc1 = lax.fori_loop(1, qi + 1, functools.partial(pair, masked=False), carry)
    o_ref[0] = _pair_output(acc0, acc1, g_ref[0]).astype(o_ref.dtype)


def _sb(zm, svt, tq):
    B, S, _ = zm.shape
    tk = svt.shape[4]
    assert tq % tk == 0
    col = lambda base: (lambda b, h, i: (b, i, base * 4 + h))
    blk = LANES
    j = jnp.arange(2 * blk, dtype=I32)[None, :] % blk
    sfx = (j >= jnp.arange(blk, dtype=I32)[:, None]).astype(BF16)
    return pl.pallas_call(
        functools.partial(_sb_kernel, tq=tq, tk=tk),
        grid=(B, 4, S // tq),
        in_specs=[pl.BlockSpec((1, tq, LANES), col(_SQ)),
                  pl.BlockSpec((1, S, LANES), lambda b, h, i: (b, 0, _SK * 4 + h)),
                  pl.BlockSpec((1, 1, S // tk, LANES, tk), lambda b, h, i: (b, h, 0, 0, 0)),
                  pl.BlockSpec((1, tq, LANES), col(_SG)),
                  pl.BlockSpec((blk, 2 * blk), lambda b, h, i: (0, 0))],
        out_specs=pl.BlockSpec((1, tq, LANES), lambda b, h, i: (b, i, h)),
        out_shape=jax.ShapeDtypeStruct((B, S, WIDTH), BF16),
        scratch_shapes=[pltpu.VMEM((2, tk, 2 * tq), F32)],
        compiler_params=_params("arbitrary", "arbitrary", "arbitrary"),
        name="stick_breaking_attention",
    )(zm, zm, svt, zm, sfx)


def _dsa_kernel(iq_ref, iqr_ref, misc_ref, cos_ref, sin_ref, dq_ref, dqr_ref, dg_ref,
                ikk_ref, kk_ref, vvt_ref, o_ref, key_ref, cut_ref, s_ref, d_ref, *, tq, tk, ktop, seq_bits):
    qi = pl.program_id(1)
    q0 = qi * tq
    nch = (q0 + tq + tk - 1) // tk
    masks = _head_masks()
    cos = cos_ref[0]
    sin = sin_ref[0]
    cos4 = jnp.concatenate([cos] * 4, axis=1)
    sin4 = jnp.concatenate([sin] * 4, axis=1)
    iq = iq_ref[0] * cos4 + iqr_ref[0] * sin4
    qd = dq_ref[0].astype(F32) * cos4 + dqr_ref[0].astype(F32) * sin4

    def slab_t(x, h):
        blk = x[:, (h // 2) * LANES:(h // 2 + 1) * LANES]
        return jnp.where(masks[h % 2], blk, 0.0).T.astype(BF16)

    iq_t = [jnp.concatenate([slab_t(iq, 2 * p), slab_t(iq, 2 * p + 1)], axis=1) for p in range(N_HEADS // 2)]
    qs_t = jnp.concatenate([slab_t(qd, h) for h in range(N_HEADS)], axis=1)
    w_t = misc_ref[0].T[N_HEADS:2 * N_HEADS, :] * (N_HEADS ** -0.5)
    w_pair = [jnp.concatenate([w_t[2 * p:2 * p + 1], w_t[2 * p + 1:2 * p + 2]], axis=1) for p in range(N_HEADS // 2)]

    key_idx0 = lax.broadcasted_iota(I32, (tk, tq), 0)
    query_idx = q0 + lax.broadcasted_iota(I32, (tk, tq), 1)

    def index_dots(c, slot):
        ik = ikk_ref[0, pl.ds(pl.multiple_of(c * tk, tk), tk), :]
        for p in range(N_HEADS // 2):
            d_ref[slot, p] = jnp.dot(ik, iq_t[p], preferred_element_type=F32)

    def score_chunk(c, slot, carry, last):
        acc2 = jnp.zeros((tk, 2 * tq), F32)
        for p in range(N_HEADS // 2):
            acc2 = acc2 + jnp.maximum(d_ref[slot, p], 0.0) * w_pair[p]
        acc = acc2[:, :tq] + acc2[:, tq:]
        acc = jnp.where(acc == 0.0, 0.0, acc)
        bits = lax.bitcast_convert_type(acc, I32)
        key = bits ^ (lax.shift_right_arithmetic(bits, 31) & 0x7FFFFFFF)
        key = jnp.maximum(key, INT_MIN + 1)
        key_ref[c] = jnp.where(c * tk + key_idx0 <= query_idx, key, INT_MIN)
        return carry

    _pipelined(nch, index_dots, score_chunk, 0)

    n_acc = 4

    slab = lambda row: jnp.broadcast_to(row, (tk, tq))

    def count(pred):
        def body(c, cnt):
            hit = jnp.where(pred(key_ref[c], c * tk + key_idx0), 1.0, 0.0)
            return cnt + hit.reshape(n_acc, tk // (8 * n_acc), 8, tq).sum(axis=1)
        cnt = lax.fori_loop(0, nch, body, jnp.zeros((n_acc, 8, tq), F32))
        return jnp.sum(cnt.reshape(n_acc * 8, tq), axis=0, keepdims=True)

    def bit_body(b, carry):
        prefix, n_prefix = carry
        cand_u = prefix | lax.shift_left(jnp.int32(1), 31 - b)
        cand = slab(cand_u ^ INT_MIN)
        n = count(lambda key, idx: key >= cand)
        keep = n >= ktop
        return jnp.where(keep, cand_u, prefix), jnp.where(keep, n, n_prefix)

    every = jnp.broadcast_to((nch * tk).astype(F32), (1, tq))
    prefix, n_ge = lax.fori_loop(0, 32, bit_body, (jnp.zeros((1, tq), I32), every))
    thr_s = slab(prefix ^ INT_MIN)
    n_gt = count(lambda key, idx: key > thr_s)
    need = ktop - n_gt

    cut_ref[...] = jnp.full(cut_ref.shape, 2 ** seq_bits, I32)

    @pl.when(jnp.max(n_ge) > ktop)
    def _():
        def idx_body(b, lo):
            cand_i = lo | lax.shift_left(jnp.int32(1), seq_bits - 1 - b)
            cand = slab(cand_i)
            g = count(lambda key, idx: (key == thr_s) & (idx < cand))
            return jnp.where(g < need, cand_i, lo)
        lo = lax.fori_loop(0, seq_bits, idx_body, jnp.zeros((1, tq), I32))
        cut_ref[...] = jnp.broadcast_to(lo, cut_ref.shape)

    thr = thr_s
    cut = slab(cut_ref[:1, :])

    def logits(c, slot):
        key = key_ref[c]
        sel = (key > thr) | ((key == thr) & (c * tk + key_idx0 <= cut))
        bias = jnp.where(sel & (key > INT_MIN), 0.0, NEG)
        k = kk_ref[0, pl.ds(pl.multiple_of(c * tk, tk), tk), :]
        s = jnp.dot(k, qs_t, preferred_element_type=F32)
        s_ref[slot] = s + jnp.concatenate([bias] * N_HEADS, axis=1)

    def consume(c, slot, carry, last):
        m, l, acc = carry
        s = s_ref[slot]
        m_new = jnp.maximum(m, jnp.max(s, axis=0, keepdims=True))
        alpha = jnp.exp2(m - m_new)
        pr = jnp.exp2(s - m_new)
        l = alpha * l + jnp.sum(pr, axis=0, keepdims=True)
        v_t = vvt_ref[0, c][:HEAD_DIM]
        acc = alpha * acc + jnp.dot(v_t, pr.astype(BF16), preferred_element_type=F32)
        return m_new, l, acc

    wide = N_HEADS * tq
    init = (jnp.full((1, wide), NEG, F32), jnp.zeros((1, wide), F32), jnp.zeros((HEAD_DIM, wide), F32))
    _, l, acc = _pipelined(nch, logits, consume, init)
    out_t = acc / l
    for p in range(N_HEADS // 2):
        g = dg_ref[0, :, p * LANES:(p + 1) * LANES]
        y = _pair_output(out_t[:, 2 * p * tq:(2 * p + 1) * tq], out_t[:, (2 * p + 1) * tq:(2 * p + 2) * tq], g)
        o_ref[0, :, p * LANES:(p + 1) * LANES] = y.astype(o_ref.dtype)


def _dsa(zs, zm, cos, sin, ikk, kk, vvt, tq, ktop):
    B, S, _ = zs.shape
    tk = vvt.shape[3]
    assert tq % LANES == 0
    wide = lambda col: pl.BlockSpec((1, tq, WIDTH), lambda b, i: (b, i, col))
    row = pl.BlockSpec((1, tq, LANES), lambda b, i: (b, i, 0))
    seq = pl.BlockSpec((1, S, LANES), lambda b, i: (b, 0, 0))
    seq_bits = max(1, (S - 1).bit_length())
    return pl.pallas_call(
        functools.partial(_dsa_kernel, tq=tq, tk=tk, ktop=ktop, seq_bits=seq_bits),
        grid=(B, S // tq),
        in_specs=[wide(_IQ), wide(_IQR),
                  pl.BlockSpec((1, tq, LANES), lambda b, i: (b, i, _MISC)),
                  row, row, wide(_DQ), wide(_DQR), wide(_DG), seq, seq,
                  pl.BlockSpec((1, S // tk, LANES, tk), lambda b, i: (b, 0, 0, 0))],
        out_specs=pl.BlockSpec((1, tq, WIDTH), lambda b, i: (b, i, 0)),
        out_shape=jax.ShapeDtypeStruct((B, S, WIDTH), BF16),
        scratch_shapes=[pltpu.VMEM((S // tk, tk, tq), I32),
                        pltpu.VMEM((8, tq), I32),
                        pltpu.VMEM((2, tk, N_HEADS * tq), F32),
                        pltpu.VMEM((2, N_HEADS // 2, tk, 2 * tq), F32)],
        compiler_params=_params("arbitrary", "arbitrary"),
        name="dsa_attention",
    )(zs, zs, zs, cos, sin, zm, zm, zm, ikk, kk, vvt)


def _merge_kernel(x_ref, gate_ref, yf_ref, yd_ref, ys_ref, m0_ref, m1_ref, m2_ref,
                  wf_ref, wd_ref, ws_ref, wo_ref, gfin_ref, o_ref, *, final):
    def branch(y_ref, w_ref, m_ref):
        proj = jnp.dot(y_ref[...], w_ref[...], preferred_element_type=F32)
        return jax.nn.sigmoid(m_ref[...].astype(F32)) * proj

    mixed = branch(yf_ref, wf_ref, m0_ref) + branch(yd_ref, wd_ref, m1_ref) + branch(ys_ref, ws_ref, m2_ref)
    o = x_ref[...] + gate_ref[0] * jnp.dot(mixed.astype(BF16), wo_ref[...], preferred_element_type=F32)
    if final:
        o = o * lax.rsqrt(jnp.mean(o * o, axis=-1, keepdims=True) + EPS) * gfin_ref[...]
    o_ref[...] = o


def _merge(x2, gate, yf, yd, ys, zm2, wf, wd, ws, wo, gfin, S, tm, final):
    M, D = x2.shape
    per_b = S // tm
    ysp = pl.BlockSpec((tm, WIDTH), lambda m: (m, 0))
    msp = lambda j: pl.BlockSpec((tm, D), lambda m: (m, j))
    wsp = pl.BlockSpec((WIDTH, D), lambda m: (0, 0))
    return pl.pallas_call(
        functools.partial(_merge_kernel, final=final),
        grid=(M // tm,),
        in_specs=[pl.BlockSpec((tm, D), lambda m: (m, 0)),
                  pl.BlockSpec((1, 1, D), lambda m: (m // per_b, 0, 0)),
                  ysp, ysp, ysp, msp(0), msp(1), msp(2), wsp, wsp, wsp,
                  pl.BlockSpec((D, D), lambda m: (0, 0)),
                  pl.BlockSpec((1, D), lambda m: (0, 0))],
        out_specs=pl.BlockSpec((tm, D), lambda m: (m, 0)),
        out_shape=jax.ShapeDtypeStruct((M, D), F32),
        compiler_params=_params("arbitrary"),
        name="merge_out",
    )(x2, gate, yf, yd, ys, zm2, zm2, zm2, wf, wd, ws, wo, gfin)


def _rot_cols(w):
    d, n = w.shape
    w4 = w.reshape(d, n // HEAD_DIM, 2, HEAD_DIM // 2)
    return jnp.concatenate([-w4[:, :, 1:], w4[:, :, :1]], axis=2).reshape(d, n)


def _split_w_in(w):
    sizes = (WIDTH, WIDTH, WIDTH, N_HEADS, WIDTH,
             WIDTH, KV_RANK, WIDTH, HEAD_DIM, N_HEADS, WIDTH,
             WIDTH, WIDTH, WIDTH, WIDTH, w.shape[1] - (11 * WIDTH + 2 * N_HEADS + KV_RANK + HEAD_DIM))
    parts, off = [], 0
    for s in sizes:
        parts.append(w[:, off:off + s])
        off += s
    return parts


def _layer_weights(w_in, w_kv_up):
    (fq, fk, fv, ff, fg, dq, dckv, diq, dik, diw, dg, sq, sk, sv, sg, merge) = _split_w_in(w_in)
    scale = HEAD_DIM ** -0.5
    fq, dq, sq, diq = fq * (scale * LOG2E), dq * (scale * LOG2E), sq * (scale * LOG2E), diq * scale
    w_main = jnp.concatenate([merge, fq, fk, fv, fg, dq, _rot_cols(dq), dg, sq, sk, sv, sg], axis=1)
    pad = jnp.zeros((w_in.shape[0], LANES - 2 * N_HEADS), w_in.dtype)
    dikr = _rot_cols(dik)
    w_small = jnp.concatenate([diq, _rot_cols(diq), ff, diw, pad, dckv, dik, dik, dikr, dikr], axis=1)
    wk, wv = w_kv_up[:, :HEAD_DIM], w_kv_up[:, HEAD_DIM:]
    wkr = _rot_cols(wk)
    w_kv = jnp.concatenate([wk, wk, wkr, wkr, wv, wv], axis=1)
    return w_main.astype(BF16), w_small.astype(BF16), w_kv.astype(BF16)


def kernel(x, c, positions, w_ada, b_ada, g_norm, w_in, b_fgt, g_kv, w_kv_up, w_br_fox, w_br_dsa, w_br_sb, w_out, g_final):
    B, S, D = x.shape
    depth = w_in.shape[0]
    assert D == 2 * WIDTH and S % 256 == 0
    ktop = min(TOPK_MAX, S // 4)
    tm = min(1024, S)
    tp = min(512, S)
    tq_att = min(512, S)
    tk_dsa = tk_fox = tp
    tk_sb = 256

    half = HEAD_DIM // 2
    inv_freq = ROPE_THETA ** (-jnp.arange(half, dtype=F32) / half)
    ang = positions.astype(F32)[..., None] * inv_freq
    cos = jnp.tile(jnp.cos(ang), (1, 1, LANES // half))
    sin = jnp.tile(jnp.sin(ang), (1, 1, LANES // half))

    mod = _modulation(c, w_ada, b_ada)
    x2 = x.reshape(B * S, D)
    for l in range(depth):
        shift = mod[l, :, None, :D]
        scale = mod[l, :, None, D:2 * D]
        gate = mod[l, :, None, 2 * D:]
        w_main, w_small, w_kv = _layer_weights(w_in[l], w_kv_up[l])
        g = g_norm[l].reshape(1, D)
        zm2 = _norm_proj(x2, g, scale, shift, w_main, S, BF16, tm, 512, "norm_proj_main")
        zs2 = _norm_proj(x2, g, scale, shift, w_small, S, F32, tm, 512, "norm_proj_small")
        zm = zm2.reshape(B, S, N_MAIN)
        zs = zs2.reshape(B, S, N_SMALL)
        bf = jnp.zeros((1, LANES), F32).at[0, :N_HEADS].set(b_fgt[l])
        fcol, kk, vvt, ikk, fvt, svt = _prep(zs, zm, cos, sin, bf, g_kv[l].reshape(1, KV_RANK), w_kv,
                                             tp, tk_dsa, tk_fox, tk_sb)
        y_fox = _fox(zm, fcol, fvt, tq_att)
        y_sb = _sb(zm, svt, tq_att)
        y_dsa = _dsa(zs, zm, cos, sin, ikk, kk, vvt, LANES, ktop)
        x2 = _merge(x2, gate, y_fox.reshape(B * S, WIDTH), y_dsa.reshape(B * S, WIDTH),
                    y_sb.reshape(B * S, WIDTH), zm2,
                    w_br_fox[l].astype(BF16), w_br_dsa[l].astype(BF16), w_br_sb[l].astype(BF16),
                    w_out[l].astype(BF16), g_final.reshape(1, D), S, min(512, S), l == depth - 1)
    return x2.reshape(B, S, D)
```
